```python
import jax, jax.numpy as jnp
from jax import lax
import numpy as np

D_MODEL = 2048
BATCH = 4
SEQ = 2048
DEPTH = 2
DEC_BATCH = 128
DEC_SEQ = 1
PAST_LEN = 16384
PAGE_SIZE = 128

N_MIXERS = 2
N_LRU_LAYERS = (DEPTH + 1) // 2
N_CONV_LAYERS = DEPTH // 2
D_RNN = (4 * D_MODEL // 3) // 128 * 128
N_LRU_BLOCKS = 16
LRU_BLOCK = D_RNN // N_LRU_BLOCKS
LRU_CONV_WIDTH = 4
LRU_C = 8.0
D_CONV = D_MODEL
CONF_CONV_WIDTH = 31
N_EXPERTS = 256
TOP_K = 8
N_GROUPS = 8
TOPK_GROUPS = 4
D_EXPERT = (D_MODEL * 2048 // 7168) // 128 * 128
D_SHARED = D_EXPERT
ROUTED_SCALE = 2.5
ROW_BLOCK = 128
N_MOD = 6
EPS = 1e-6

kernel_name = "hybrid_rglru_conformer_moe_adaln_step"


def _rmsnorm(x, g):
    x32 = x.astype(jnp.float32)
    y = x32 * lax.rsqrt(jnp.mean(x32 * x32, axis=-1, keepdims=True) + EPS)
    return (y * g.astype(jnp.float32)).astype(x.dtype)


def _layernorm(x, g, b):
    x32 = x.astype(jnp.float32)
    mu = jnp.mean(x32, axis=-1, keepdims=True)
    d = x32 - mu
    var = jnp.mean(d * d, axis=-1, keepdims=True)
    return (d * lax.rsqrt(var + EPS) * g.astype(jnp.float32) + b.astype(jnp.float32)).astype(x.dtype)


def _causal_dwconv(x, buf, w, b):
    width, ch = w.shape
    xp = jnp.concatenate([buf.astype(x.dtype), x], axis=1)
    y = lax.conv_general_dilated(xp, w[:, None, :].astype(x.dtype), window_strides=(1,), padding='VALID',
                                 dimension_numbers=('NWC', 'WIO', 'NWC'), feature_group_count=ch)
    return y + b, xp[:, xp.shape[1] - (width - 1):]


def _rglru(x, h0, w_a, b_a, w_x, b_x, lam):
    B, T, _ = x.shape
    xh = x.reshape(B, T, N_LRU_BLOCKS, LRU_BLOCK)
    r = jax.nn.sigmoid((jnp.einsum('btnd,nde->btne', xh, w_a).reshape(B, T, D_RNN) + b_a).astype(jnp.float32))
    i = jax.nn.sigmoid((jnp.einsum('btnd,nde->btne', xh, w_x).reshape(B, T, D_RNN) + b_x).astype(jnp.float32))
    log_a = -LRU_C * r * jax.nn.softplus(-lam.astype(jnp.float32))
    a = jnp.exp(log_a)
    u = jnp.sqrt(-jnp.expm1(2.0 * log_a)) * (i * x.astype(jnp.float32))
    u = u.at[:, 0].add(a[:, 0] * h0.astype(jnp.float32))

    def combine(p, q):
        return (p[0] * q[0], q[0] * p[1] + q[1])

    _, h = lax.associative_scan(combine, (a, u), axis=1)
    return h.astype(x.dtype), h[:, -1].astype(h0.dtype)


def _recurrent_block(h, conv_buf, h0, w_in, w_conv, b_conv, w_a, b_a, w_x, b_x, lam, w_out):
    z = h @ w_in
    gate_branch, main = jnp.split(z, 2, axis=-1)
    gate = jax.nn.gelu(gate_branch, approximate=True)
    main, new_buf = _causal_dwconv(main, conv_buf, w_conv, b_conv)
    y, h_last = _rglru(main, h0, w_a, b_a, w_x, b_x, lam)
    return (y * gate) @ w_out, new_buf, h_last


def _conformer_conv(h, buf, w_pw1, b_pw1, w_dw, b_dw, g_ln, b_ln, w_pw2, b_pw2):
    z = h @ w_pw1 + b_pw1
    a, g = jnp.split(z, 2, axis=-1)
    u = a * jax.nn.sigmoid(g)
    v, new_buf = _causal_dwconv(u, buf, w_dw, b_dw)
    v = jax.nn.silu(_layernorm(v, g_ln, b_ln))
    return v @ w_pw2 + b_pw2, new_buf


def _moe(h2, w_router, b_router, w_eg, w_eu, w_ed, w_sg, w_su, w_sd):
    N, D = h2.shape
    probs = jax.nn.sigmoid(h2.astype(jnp.float32) @ w_router.astype(jnp.float32))
    choice = probs + b_router.astype(jnp.float32)
    grp = choice.reshape(N, N_GROUPS, N_EXPERTS // N_GROUPS)
    grp_score = jnp.sum(lax.top_k(grp, 2)[0], axis=-1)
    _, top_grp = lax.top_k(grp_score, TOPK_GROUPS)
    grp_mask = jnp.any(top_grp[:, :, None] == jnp.arange(N_GROUPS)[None, None, :], axis=1)
    masked = jnp.where(jnp.repeat(grp_mask, N_EXPERTS // N_GROUPS, axis=1), choice, -jnp.inf)
    _, idx = lax.top_k(masked, TOP_K)
    wts = jnp.take_along_axis(probs, idx, axis=1)
    wts = wts / jnp.sum(wts, axis=-1, keepdims=True) * ROUTED_SCALE

    n_assign = N * TOP_K
    e_flat = idx.reshape(-1)
    t_flat = jnp.repeat(jnp.arange(N, dtype=jnp.int32), TOP_K)
    g_flat = wts.reshape(-1)
    order = jnp.argsort(e_flat)
    e_sorted = e_flat[order]
    sizes = jnp.bincount(e_flat, length=N_EXPERTS)
    starts = jnp.cumsum(sizes) - sizes
    padded = (sizes + ROW_BLOCK - 1) // ROW_BLOCK * ROW_BLOCK
    pstarts = jnp.cumsum(padded) - padded
    pends = pstarts + padded
    dest = pstarts[e_sorted] + (jnp.arange(n_assign) - starts[e_sorted])
    n_blocks = -(-n_assign // ROW_BLOCK) + N_EXPERTS
    n_rows = n_blocks * ROW_BLOCK
    row_tok = jnp.zeros((n_rows,), jnp.int32).at[dest].set(t_flat[order])
    row_gate = jnp.zeros((n_rows,), h2.dtype).at[dest].set(g_flat[order].astype(h2.dtype))
    blk_start = jnp.arange(n_blocks) * ROW_BLOCK
    blk_exp = jnp.minimum(jnp.sum(pends[None, :] <= blk_start[:, None], axis=1), N_EXPERTS - 1)

    def expert_block(args):
        tok, gate, e = args
        xb = h2[tok]
        act = jax.nn.silu(xb @ w_eg[e]) * (xb @ w_eu[e])
        return (act @ w_ed[e]) * gate[:, None]

    y_rows = lax.map(expert_block, (row_tok.reshape(n_blocks, ROW_BLOCK),
                                    row_gate.reshape(n_blocks, ROW_BLOCK), blk_exp))
    routed = jax.ops.segment_sum(y_rows.reshape(n_rows, D), row_tok, num_segments=N)
    shared = (jax.nn.silu(h2 @ w_sg) * (h2 @ w_su)) @ w_sd
    return routed + shared


def _trunk(x, c, lru_h, lru_conv, conv_buf, P):
    B, T, D = x.shape
    sc = jax.nn.silu(c)
    hs, lcs, cbs = [], [], []
    for l in range(DEPTH):
        mod = (sc @ P['w_ada'][l] + P['b_ada'][l]).reshape(B, N_MOD, 1, D)
        h = _rmsnorm(x, P['g_mix'][l]) * (1.0 + mod[:, 1]) + mod[:, 0]
        j = l // N_MIXERS
        if l % N_MIXERS == 0:
            out, lc, hl = _recurrent_block(h, lru_conv[j], lru_h[j], P['w_lru_in'][j], P['w_lru_conv'][j],
                                           P['b_lru_conv'][j], P['w_lru_a'][j], P['b_lru_a'][j],
                                           P['w_lru_x'][j], P['b_lru_x'][j], P['lru_lambda'][j], P['w_lru_out'][j])
            hs.append(hl)
            lcs.append(lc)
        else:
            out, cb = _conformer_conv(h, conv_buf[j], P['w_pw1'][j], P['b_pw1'][j], P['w_dw'][j], P['b_dw'][j],
                                      P['g_cln'][j], P['b_cln'][j], P['w_pw2'][j], P['b_pw2'][j])
            cbs.append(cb)
        x = x + mod[:, 2] * out
        h = _rmsnorm(x, P['g_ffn'][l]) * (1.0 + mod[:, 4]) + mod[:, 3]
        f = _moe(h.reshape(B * T, D), P['w_router'][l], P['b_router'][l], P['w_exp_gate'][l], P['w_exp_up'][l],
                 P['w_exp_down'][l], P['w_sh_gate'][l], P['w_sh_up'][l], P['w_sh_down'][l]).reshape(B, T, D)
        x = x + mod[:, 5] * f
    return _rmsnorm(x, P['g_final']), jnp.stack(hs), jnp.stack(lcs), jnp.stack(cbs)


def setup_inputs(seed: int = 0) -> dict:
    key = jax.random.key(seed)
    ks = iter(jax.random.split(key, 64))

    def nrm(shape, scale):
        return jax.random.normal(next(ks), shape, jnp.float32) * scale

    NA, NC, D = N_LRU_LAYERS, N_CONV_LAYERS, D_MODEL
    a0 = jax.random.uniform(next(ks), (NA, D_RNN), jnp.float32, 0.9, 0.999) ** (1.0 / LRU_C)
    return {
        'x_prompt': nrm((BATCH, SEQ, D), 1.0),
        'x_sample': nrm((DEC_BATCH, DEC_SEQ, D), 1.0),
        'state_lru_h': nrm((NA, DEC_BATCH, D_RNN), 0.5),
        'state_lru_conv': nrm((NA, DEC_BATCH, LRU_CONV_WIDTH - 1, D_RNN), 1.0),
        'state_conv': nrm((NC, DEC_BATCH, CONF_CONV_WIDTH - 1, D_CONV), 0.5),
        'c_prompt': nrm((BATCH, D), 1.0),
        'c_sample': nrm((DEC_BATCH, D), 1.0),
        'w_ada': nrm((DEPTH, D, N_MOD * D), 0.5 * D ** -0.5),
        'b_ada': nrm((DEPTH, N_MOD * D), 0.02),
        'g_mix': 1.0 + nrm((DEPTH, D), 0.02),
        'g_ffn': 1.0 + nrm((DEPTH, D), 0.02),
        'w_lru_in': nrm((NA, D, 2 * D_RNN), D ** -0.5),
        'w_lru_conv': nrm((NA, LRU_CONV_WIDTH, D_RNN), LRU_CONV_WIDTH ** -0.5),
        'b_lru_conv': nrm((NA, D_RNN), 0.02),
        'w_lru_a': nrm((NA, N_LRU_BLOCKS, LRU_BLOCK, LRU_BLOCK), LRU_BLOCK ** -0.5),
        'b_lru_a': nrm((NA, D_RNN), 0.02),
        'w_lru_x': nrm((NA, N_LRU_BLOCKS, LRU_BLOCK, LRU_BLOCK), LRU_BLOCK ** -0.5),
        'b_lru_x': nrm((NA, D_RNN), 0.02),
        'lru_lambda': jnp.log(a0) - jnp.log1p(-a0),
        'w_lru_out': nrm((NA, D_RNN, D), D_RNN ** -0.5),
        'w_pw1': nrm((NC, D, 2 * D_CONV), D ** -0.5),
        'b_pw1': nrm((NC, 2 * D_CONV), 0.02),
        'w_dw': nrm((NC, CONF_CONV_WIDTH, D_CONV), CONF_CONV_WIDTH ** -0.5),
        'b_dw': nrm((NC, D_CONV), 0.02),
        'g_cln': 1.0 + nrm((NC, D_CONV), 0.02),
        'b_cln': nrm((NC, D_CONV), 0.02),
        'w_pw2': nrm((NC, D_CONV, D), D_CONV ** -0.5),
        'b_pw2': nrm((NC, D), 0.02),
        'w_router': nrm((DEPTH, D, N_EXPERTS), D ** -0.5),
        'b_router': nrm((DEPTH, N_EXPERTS), 0.01),
        'w_exp_gate': nrm((DEPTH, N_EXPERTS, D, D_EXPERT), D ** -0.5),
        'w_exp_up': nrm((DEPTH, N_EXPERTS, D, D_EXPERT), D ** -0.5),
        'w_exp_down': nrm((DEPTH, N_EXPERTS, D_EXPERT, D), D_EXPERT ** -0.5),
        'w_sh_gate': nrm((DEPTH, D, D_SHARED), D ** -0.5),
        'w_sh_up': nrm((DEPTH, D, D_SHARED), D ** -0.5),
        'w_sh_down': nrm((DEPTH, D_SHARED, D), D_SHARED ** -0.5),
        'g_final': 1.0 + nrm((D,), 0.02),
    }


def reference(x_prompt, x_sample, state_lru_h, state_lru_conv, state_conv, c_prompt, c_sample,
              w_ada, b_ada, g_mix, g_ffn, w_lru_in, w_lru_conv, b_lru_conv, w_lru_a, b_lru_a, w_lru_x, b_lru_x,
              lru_lambda, w_lru_out, w_pw1, b_pw1, w_dw, b_dw, g_cln, b_cln, w_pw2, b_pw2,
              w_router, b_router, w_exp_gate, w_exp_up, w_exp_down, w_sh_gate, w_sh_up, w_sh_down, g_final):
    P = dict(w_ada=w_ada, b_ada=b_ada, g_mix=g_mix, g_ffn=g_ffn, w_lru_in=w_lru_in, w_lru_conv=w_lru_conv,
             b_lru_conv=b_lru_conv, w_lru_a=w_lru_a, b_lru_a=b_lru_a, w_lru_x=w_lru_x, b_lru_x=b_lru_x,
             lru_lambda=lru_lambda, w_lru_out=w_lru_out, w_pw1=w_pw1, b_pw1=b_pw1, w_dw=w_dw, b_dw=b_dw,
             g_cln=g_cln, b_cln=b_cln, w_pw2=w_pw2, b_pw2=b_pw2, w_router=w_router, b_router=b_router,
             w_exp_gate=w_exp_gate, w_exp_up=w_exp_up, w_exp_down=w_exp_down, w_sh_gate=w_sh_gate,
             w_sh_up=w_sh_up, w_sh_down=w_sh_down, g_final=g_final)
    B = x_prompt.shape[0]
    h0_p = jnp.zeros((N_LRU_LAYERS, B, D_RNN), state_lru_h.dtype)
    lc0_p = jnp.zeros((N_LRU_LAYERS, B, LRU_CONV_WIDTH - 1, D_RNN), state_lru_conv.dtype)
    cb0_p = jnp.zeros((N_CONV_LAYERS, B, CONF_CONV_WIDTH - 1, D_CONV), state_conv.dtype)
    y_prompt, h_p, lc_p, cb_p = _trunk(x_prompt, c_prompt, h0_p, lc0_p, cb0_p, P)
    y_sample, h_s, lc_s, cb_s = _trunk(x_sample, c_sample, state_lru_h, state_lru_conv, state_conv, P)
    return (y_prompt, y_sample, h_p, lc_p, cb_p, h_s, lc_s, cb_s)
```

```python
import functools

import jax
import jax.numpy as jnp
from jax import lax
from jax.experimental import pallas as pl
from jax.experimental.pallas import tpu as pltpu

F32 = jnp.float32
BF16 = jnp.bfloat16
I32 = jnp.int32
U32 = jnp.uint32

N_LRU_BLOCKS = 16
LRU_C = 8.0
TOP_K = 8
N_GROUPS = 8
TOPK_GROUPS = 4
ROUTED_SCALE = 2.5
ROW_BLOCK = 128
N_MOD = 6
EPS = 1e-6

LANES = 128
SUBLANES = 8
GATE_WINDOW = 512
V7X_VMEM_BYTES = 64 * 1024 * 1024
VMEM_LIMIT = V7X_VMEM_BYTES - 8 * 1024 * 1024


def _pick(n, prefs):
    for p in prefs:
        if n % p == 0:
            return p
    return n


def _cparams(sem):
    return pltpu.CompilerParams(dimension_semantics=sem, vmem_limit_bytes=VMEM_LIMIT)


def _dot(a, b):
    return jnp.dot(a, b, preferred_element_type=F32)


def _sigmoid(x):
    return jax.nn.sigmoid(x)


def _silu(x):
    return x * jax.nn.sigmoid(x)


def _pack_bf16_pairs(h):
    half = h.shape[1] // 2
    bits = lax.bitcast_convert_type(h.astype(BF16).astype(F32), U32)
    return (bits[:, :half] >> 16) | (bits[:, half:] & jnp.uint32(0xFFFF0000))


def _unpack_bf16_pairs(w):
    lo = lax.bitcast_convert_type(w << 16, F32).astype(BF16)
    hi = lax.bitcast_convert_type(w & jnp.uint32(0xFFFF0000), F32).astype(BF16)
    return jnp.concatenate([lo, hi], axis=1)


def _ada_kernel(c_ref, w_ref, b_ref, o_ref):
    a = _silu(c_ref[...]).astype(BF16)
    o_ref[...] = _dot(a, w_ref[...].astype(BF16)) + b_ref[...]


def _ada_mod(c_all, w_ada, b_ada):
    depth, d, n = w_ada.shape
    bp = c_all.shape[0]
    tn = _pick(n, (1536, 1024, 512, 256, 128))
    return pl.pallas_call(
        _ada_kernel,
        grid=(depth, n // tn),
        in_specs=[
            pl.BlockSpec((bp, d), lambda l, j: (0, 0)),
            pl.BlockSpec((None, d, tn), lambda l, j: (l, 0, j)),
            pl.BlockSpec((None, 1, tn), lambda l, j: (l, 0, j)),
        ],
        out_specs=pl.BlockSpec((None, bp, tn), lambda l, j: (l, 0, j)),
        out_shape=jax.ShapeDtypeStruct((depth, bp, n), F32),
        compiler_params=_cparams(("arbitrary", "arbitrary")),
    )(c_all, w_ada, b_ada.reshape(depth, 1, n))


def _mod_spec(seq, tm, tn, col_axis_first):
    if seq > 1:
        shape = (None, 1, tn)
        if col_axis_first:
            return pl.BlockSpec(shape, lambda n, i: ((i * tm) // seq, 0, n))
        return pl.BlockSpec(shape, lambda i: ((i * tm) // seq, 0, 0))
    shape = (None, tm, tn)
    if col_axis_first:
        return pl.BlockSpec(shape, lambda n, i: (0, i, n))
    return pl.BlockSpec(shape, lambda i: (0, i, 0))


def _norm_mod_kernel(x_ref, g_ref, sh_ref, sc_ref, o_ref, *, pack):
    x = x_ref[...]
    y = x * lax.rsqrt(jnp.mean(x * x, axis=-1, keepdims=True) + EPS)
    h = (y * g_ref[...]) * (1.0 + sc_ref[...]) + sh_ref[...]
    if pack:
        o_ref[...] = _pack_bf16_pairs(h)
    else:
        o_ref[...] = h.astype(BF16)


def _norm_mod(x, g, shift, scale, seq, pack):
    m, d = x.shape
    tm = _pick(m if seq == 1 else seq, (512, 256, 128, 64, 32, 16, 8))
    out_d, out_t = (d // 2, U32) if pack else (d, BF16)
    return pl.pallas_call(
        functools.partial(_norm_mod_kernel, pack=pack),
        grid=(m // tm,),
        in_specs=[
            pl.BlockSpec((tm, d), lambda i: (i, 0)),
            pl.BlockSpec((1, d), lambda i: (0, 0)),
            _mod_spec(seq, tm, d, False),
            _mod_spec(seq, tm, d, False),
        ],
        out_specs=pl.BlockSpec((tm, out_d), lambda i: (i, 0)),
        out_shape=jax.ShapeDtypeStruct((m, out_d), out_t),
        compiler_params=_cparams(("arbitrary",)),
    )(x, g.reshape(1, d), shift, scale)


def _final_norm_kernel(x_ref, g_ref, o_ref):
    x = x_ref[...]
    y = x * lax.rsqrt(jnp.mean(x * x, axis=-1, keepdims=True) + EPS)
    o_ref[...] = y * g_ref[...]


def _final_norm(x, g):
    m, d = x.shape
    tm = _pick(m, (512, 256, 128, 64, 32, 16, 8))
    return pl.pallas_call(
        _final_norm_kernel,
        grid=(m // tm,),
        in_specs=[pl.BlockSpec((tm, d), lambda i: (i, 0)), pl.BlockSpec((1, d), lambda i: (0, 0))],
        out_specs=pl.BlockSpec((tm, d), lambda i: (i, 0)),
        out_shape=jax.ShapeDtypeStruct((m, d), F32),
        compiler_params=_cparams(("arbitrary",)),
    )(x, g.reshape(1, d))


def _mm_plain_kernel(a_ref, w_ref, o_ref, wb_ref):
    @pl.when(pl.program_id(1) == 0)
    def _():
        wb_ref[...] = w_ref[...].astype(BF16)

    o_ref[...] = _dot(a_ref[...], wb_ref[...])


def _mm_plain(a, w, layer, tn_prefs):
    m, k = a.shape
    n = w.shape[2]
    tm = _pick(m, (512, 256, 128, 64, 32, 16))
    tn = _pick(n, tn_prefs)
    return pl.pallas_call(
        _mm_plain_kernel,
        grid=(n // tn, m // tm),
        in_specs=[
            pl.BlockSpec((tm, k), lambda j, i: (i, 0)),
            pl.BlockSpec((None, k, tn), lambda j, i: (layer, 0, j)),
        ],
        out_specs=pl.BlockSpec((tm, tn), lambda j, i: (i, j)),
        out_shape=jax.ShapeDtypeStruct((m, n), F32),
        scratch_shapes=[pltpu.VMEM((k, tn), BF16)],
        compiler_params=_cparams(("arbitrary", "arbitrary")),
    )(a, w)


def _mm_glu_kernel(a_ref, wa_ref, wg_ref, ba_ref, bg_ref, o_ref, wab_ref, wgb_ref):
    @pl.when(pl.program_id(1) == 0)
    def _():
        wab_ref[...] = wa_ref[...].astype(BF16)
        wgb_ref[...] = wg_ref[...].astype(BF16)

    a = a_ref[...]
    va = _dot(a, wab_ref[...]) + ba_ref[...]
    vg = _dot(a, wgb_ref[...]) + bg_ref[...]
    o_ref[...] = va * _sigmoid(vg)


def _mm_glu(a, w, b, layer):
    m, k = a.shape
    n = w.shape[2] // 2
    tm = _pick(m, (512, 256, 128, 64, 32, 16))
    tn = _pick(n, (512, 256, 128))
    nj = n // tn
    b3 = b.reshape(b.shape[0], 1, 2 * n)
    return pl.pallas_call(
        _mm_glu_kernel,
        grid=(nj, m // tm),
        in_specs=[
            pl.BlockSpec((tm, k), lambda j, i: (i, 0)),
            pl.BlockSpec((None, k, tn), lambda j, i: (layer, 0, j)),
            pl.BlockSpec((None, k, tn), lambda j, i: (layer, 0, j + nj)),
            pl.BlockSpec((None, 1, tn), lambda j, i: (layer, 0, j)),
            pl.BlockSpec((None, 1, tn), lambda j, i: (layer, 0, j + nj)),
        ],
        out_specs=pl.BlockSpec((tm, tn), lambda j, i: (i, j)),
        out_shape=jax.ShapeDtypeStruct((m, n), F32),
        scratch_shapes=[pltpu.VMEM((k, tn), BF16), pltpu.VMEM((k, tn), BF16)],
        compiler_params=_cparams(("arbitrary", "arbitrary")),
    )(a, w, w, b3, b3)


def _mm_res_kernel(a_ref, w_ref, b_ref, x_ref, gate_ref, o_ref, wb_ref):
    @pl.when(pl.program_id(1) == 0)
    def _():
        wb_ref[...] = w_ref[...].astype(BF16)

    out = _dot(a_ref[...], wb_ref[...]) + b_ref[...]
    o_ref[...] = x_ref[...] + gate_ref[...] * out


def _mm_res(a, w, b, x, gate, layer, seq):
    m, k = a.shape
    n = w.shape[2]
    tm = _pick(m if seq == 1 else seq, (512, 256, 128, 64, 32, 16))
    tn = _pick(n, (512, 256, 128))
    return pl.pallas_call(
        _mm_res_kernel,
        grid=(n // tn, m // tm),
        in_specs=[
            pl.BlockSpec((tm, k), lambda j, i: (i, 0)),
            pl.BlockSpec((None, k, tn), lambda j, i: (layer, 0, j)),
            pl.BlockSpec((1, tn), lambda j, i: (0, j)),
            pl.BlockSpec((tm, tn), lambda j, i: (i, j)),
            _mod_spec(seq, tm, tn, True),
        ],
        out_specs=pl.BlockSpec((tm, tn), lambda j, i: (i, j)),
        out_shape=jax.ShapeDtypeStruct((m, n), F32),
        scratch_shapes=[pltpu.VMEM((k, tn), BF16)],
        compiler_params=_cparams(("arbitrary", "arbitrary")),
    )(a, w, b.reshape(1, n), x, gate)


def _gate_windows(d_rnn, bs):
    k0s = []
    for j in range(d_rnn // LANES):
        c0, c1 = LANES * j, LANES * (j + 1)
        n_lo, n_hi = c0 // bs, (c1 - 1) // bs
        k0 = min((n_lo * bs) // LANES * LANES, d_rnn - GATE_WINDOW)
        assert k0 <= n_lo * bs and (n_hi + 1) * bs <= k0 + GATE_WINDOW
        k0s.append(k0)
    return k0s


def _pack_gate_weights(w_a, w_x):
    nb, bs, _ = w_a.shape
    d_rnn = nb * bs
    k0s = _gate_windows(d_rnn, bs)
    tiles = []
    for j, k0 in enumerate(k0s):
        c0, c1 = LANES * j, LANES * (j + 1)
        tile = jnp.zeros((GATE_WINDOW, 2 * LANES), F32)
        for n in range(c0 // bs, (c1 - 1) // bs + 1):
            cs, ce = max(c0, n * bs), min(c1, (n + 1) * bs)
            r0 = n * bs - k0
            tile = tile.at[r0:r0 + bs, cs - c0:ce - c0].set(w_a[n, :, cs - n * bs:ce - n * bs])
            tile = tile.at[r0:r0 + bs, LANES + cs - c0:LANES + ce - c0].set(w_x[n, :, cs - n * bs:ce - n * bs])
        tiles.append(tile)
    return jnp.stack(tiles).astype(BF16)


def _softplus(x):
    return jnp.maximum(x, 0.0) + jnp.log1p(jnp.exp(-jnp.abs(x)))


def _gelu_tanh(x):
    c = 0.7978845608028654
    return 0.5 * x * (1.0 + jnp.tanh(c * (x + 0.044715 * (x * x * x))))


def _lru_gate_tile(xc, xb, wp_ref, ba_ref, bx_ref, sp, j, k0):
    c0, c1 = LANES * j, LANES * (j + 1)
    pre = _dot(xb[:, k0:k0 + GATE_WINDOW], wp_ref[j])
    r = _sigmoid(pre[:, :LANES] + ba_ref[:, c0:c1])
    i = _sigmoid(pre[:, LANES:] + bx_ref[:, c0:c1])
    log_a = (-LRU_C * r) * sp[:, c0:c1]
    a = jnp.exp(log_a)
    u = jnp.sqrt(1.0 - jnp.exp(2.0 * log_a)) * (i * xc[:, c0:c1])
    return a, u


def _lru_seq_kernel(gb_ref, main_ref, wc_ref, bc_ref, wp_ref, ba_ref, bx_ref, lam_ref,
                    yg_ref, hl_ref, xp_ref, xc_ref, xb_ref, hc_ref, *, tt, k0s):
    t = pl.program_id(1)
    d_rnn = main_ref.shape[1]

    @pl.when(t == 0)
    def _():
        xp_ref[0:SUBLANES, :] = jnp.zeros((SUBLANES, d_rnn), F32)
        hc_ref[...] = jnp.zeros_like(hc_ref)

    xp_ref[SUBLANES:, :] = main_ref[...]
    acc = bc_ref[...] + wc_ref[3:4, :] * xp_ref[pl.ds(SUBLANES, tt), :]
    for k in range(3):
        acc = acc + wc_ref[k:k + 1, :] * xp_ref[pl.ds(SUBLANES - 3 + k, tt), :]
    xc_ref[...] = acc
    xb_ref[...] = acc.astype(BF16)
    xp_ref[0:SUBLANES, :] = xp_ref[tt:tt + SUBLANES, :]

    sp = _softplus(-lam_ref[...])
    row = lax.broadcasted_iota(I32, (tt, LANES), 0)
    for j, k0 in enumerate(k0s):
        c0, c1 = LANES * j, LANES * (j + 1)
        a, u = _lru_gate_tile(xc_ref, xb_ref, wp_ref, ba_ref, bx_ref, sp, j, k0)
        d = 1
        while d < tt:
            keep = row >= d
            a_s = jnp.where(keep, pltpu.roll(a, d, 0), 1.0)
            u_s = jnp.where(keep, pltpu.roll(u, d, 0), 0.0)
            u = a * u_s + u
            a = a * a_s
            d *= 2
        h = a * hc_ref[:, c0:c1] + u
        hc_ref[:, c0:c1] = h[tt - 1:tt, :]
        yg_ref[:, c0:c1] = (h * _gelu_tanh(gb_ref[:, c0:c1])).astype(BF16)

    @pl.when(t == pl.num_programs(1) - 1)
    def _():
        hl_ref[...] = hc_ref[...]


def _lru_seq(z, batch, seq, w_conv, b_conv, wp, b_a, b_x, lam, k0s):
    d_rnn = z.shape[1] // 2
    tt = _pick(seq, (256, 128, 64, 32, 16, 8))
    nt = seq // tt
    row = lambda b, t: b * nt + t
    full = lambda shape: pl.BlockSpec(shape, lambda b, t: (0,) * len(shape))
    yg, h_last = pl.pallas_call(
        functools.partial(_lru_seq_kernel, tt=tt, k0s=k0s),
        grid=(batch, nt),
        in_specs=[
            pl.BlockSpec((tt, d_rnn), lambda b, t: (row(b, t), 0)),
            pl.BlockSpec((tt, d_rnn), lambda b, t: (row(b, t), 1)),
            full((4, d_rnn)), full((1, d_rnn)), full(wp.shape),
            full((1, d_rnn)), full((1, d_rnn)), full((1, d_rnn)),
        ],
        out_specs=[
            pl.BlockSpec((tt, d_rnn), lambda b, t: (row(b, t), 0)),
            pl.BlockSpec((None, 1, d_rnn), lambda b, t: (b, 0, 0)),
        ],
        out_shape=[
            jax.ShapeDtypeStruct((batch * seq, d_rnn), BF16),
            jax.ShapeDtypeStruct((batch, 1, d_rnn), F32),
        ],
        scratch_shapes=[
            pltpu.VMEM((tt + SUBLANES, d_rnn), F32),
            pltpu.VMEM((tt, d_rnn), F32),
            pltpu.VMEM((tt, d_rnn), BF16),
            pltpu.VMEM((1, d_rnn), F32),
        ],
        compiler_params=_cparams(("arbitrary", "arbitrary")),
    )(z, z, w_conv, b_conv.reshape(1, -1), wp, b_a.reshape(1, -1), b_x.reshape(1, -1), lam.reshape(1, -1))
    return yg, h_last.reshape(batch, d_rnn)


def _lru_step_kernel(gb_ref, main_ref, p0_ref, p1_ref, p2_ref, h0_ref, wc_ref, bc_ref, wp_ref, ba_ref,
                     bx_ref, lam_ref, yg_ref, h_ref, xc_ref, xb_ref, *, k0s):
    acc = (bc_ref[...] + wc_ref[3:4, :] * main_ref[...] + wc_ref[2:3, :] * p2_ref[...]
           + wc_ref[1:2, :] * p1_ref[...] + wc_ref[0:1, :] * p0_ref[...])
    xc_ref[...] = acc
    xb_ref[...] = acc.astype(BF16)
    sp = _softplus(-lam_ref[...])
    for j, k0 in enumerate(k0s):
        c0, c1 = LANES * j, LANES * (j + 1)
        a, u = _lru_gate_tile(xc_ref, xb_ref, wp_ref, ba_ref, bx_ref, sp, j, k0)
        h = a * h0_ref[:, c0:c1] + u
        h_ref[:, c0:c1] = h
        yg_ref[:, c0:c1] = (h * _gelu_tanh(gb_ref[:, c0:c1])).astype(BF16)


def _lru_step(z, prev, h0, w_conv, b_conv, wp, b_a, b_x, lam, k0s):
    m = z.shape[0]
    d_rnn = z.shape[1] // 2
    tm = _pick(m, (128, 64, 32, 16))
    rows = lambda c: pl.BlockSpec((tm, d_rnn), lambda i: (i, c))
    full = lambda shape: pl.BlockSpec(shape, lambda i: (0,) * len(shape))
    return pl.pallas_call(
        functools.partial(_lru_step_kernel, k0s=k0s),
        grid=(m // tm,),
        in_specs=[
            rows(0), rows(1), rows(0), rows(0), rows(0), rows(0),
            full((4, d_rnn)), full((1, d_rnn)), full(wp.shape),
            full((1, d_rnn)), full((1, d_rnn)), full((1, d_rnn)),
        ],
        out_specs=[rows(0), rows(0)],
        out_shape=[jax.ShapeDtypeStruct((m, d_rnn), BF16), jax.ShapeDtypeStruct((m, d_rnn), F32)],
        scratch_shapes=[pltpu.VMEM((tm, d_rnn), F32), pltpu.VMEM((tm, d_rnn), BF16)],
        compiler_params=_cparams(("arbitrary",)),
    )(z, z, prev[0], prev[1], prev[2], h0, w_conv, b_conv.reshape(1, -1), wp,
      b_a.reshape(1, -1), b_x.reshape(1, -1), lam.reshape(1, -1))


def _ln_silu(v, g, b):
    mu = jnp.mean(v, axis=-1, keepdims=True)
    dv = v - mu
    var = jnp.mean(dv * dv, axis=-1, keepdims=True)
    return _silu(dv * lax.rsqrt(var + EPS) * g + b)


def _conf_seq_kernel(u_ref, w_ref, b_ref, g_ref, bl_ref, o_ref, xp_ref, v_ref, *, tt, width, pad):
    d = u_ref.shape[1]

    @pl.when(pl.program_id(1) == 0)
    def _():
        xp_ref[0:pad, :] = jnp.zeros((pad, d), F32)

    xp_ref[pad:, :] = u_ref[...]
    base = pad - (width - 1)
    for c in range(d // LANES):
        c0, c1 = LANES * c, LANES * (c + 1)
        acc = b_ref[:, c0:c1] + w_ref[0:1, c0:c1] * xp_ref[pl.ds(base, tt), c0:c1]
        for k in range(1, width):
            acc = acc + w_ref[k:k + 1, c0:c1] * xp_ref[pl.ds(base + k, tt), c0:c1]
        v_ref[:, c0:c1] = acc
    xp_ref[0:pad, :] = xp_ref[tt:tt + pad, :]
    o_ref[...] = _ln_silu(v_ref[...], g_ref[...], bl_ref[...]).astype(BF16)


def _conf_seq(u, batch, seq, w_dw, b_dw, g_ln, b_ln):
    d = u.shape[1]
    width = w_dw.shape[0]
    pad = -(-(width - 1) // SUBLANES) * SUBLANES
    tt = _pick(seq, (256, 128, 64, 32))
    nt = seq // tt
    full = lambda shape: pl.BlockSpec(shape, lambda b, t: (0,) * len(shape))
    return pl.pallas_call(
        functools.partial(_conf_seq_kernel, tt=tt, width=width, pad=pad),
        grid=(batch, nt),
        in_specs=[
            pl.BlockSpec((tt, d), lambda b, t: (b * nt + t, 0)),
            full((width, d)), full((1, d)), full((1, d)), full((1, d)),
        ],
        out_specs=pl.BlockSpec((tt, d), lambda b, t: (b * nt + t, 0)),
        out_shape=jax.ShapeDtypeStruct((batch * seq, d), BF16),
        scratch_shapes=[pltpu.VMEM((tt + pad, d), F32), pltpu.VMEM((tt, d), F32)],
        compiler_params=_cparams(("arbitrary", "arbitrary")),
    )(u, w_dw, b_dw.reshape(1, d), g_ln.reshape(1, d), b_ln.reshape(1, d))


def _conf_step_kernel(u_ref, buf_ref, w_ref, b_ref, g_ref, bl_ref, o_ref, *, width):
    acc = b_ref[...] + w_ref[width - 1:width, :] * u_ref[...]
    for k in range(width - 1):
        acc = acc + w_ref[k:k + 1, :] * buf_ref[k]
    o_ref[...] = _ln_silu(acc, g_ref[...], bl_ref[...]).astype(BF16)


def _conf_step(u, buf_t, w_dw, b_dw, g_ln, b_ln):
    m, d = u.shape
    width = w_dw.shape[0]
    tm = _pick(m, (32, 16, 8))
    full = lambda shape: pl.BlockSpec(shape, lambda i: (0,) * len(shape))
    return pl.pallas_call(
        functools.partial(_conf_step_kernel, width=width),
        grid=(m // tm,),
        in_specs=[
            pl.BlockSpec((tm, d), lambda i: (i, 0)),
            pl.BlockSpec((width - 1, tm, d), lambda i: (0, i, 0)),
            full((width, d)), full((1, d)), full((1, d)), full((1, d)),
        ],
        out_specs=pl.BlockSpec((tm, d), lambda i: (i, 0)),
        out_shape=jax.ShapeDtypeStruct((m, d), BF16),
        compiler_params=_cparams(("arbitrary",)),
    )(u, buf_t, w_dw, b_dw.reshape(1, d), g_ln.reshape(1, d), b_ln.reshape(1, d))


def _lane_dense(cols, tm):
    lane = lax.broadcasted_iota(I32, (tm, LANES), 1)
    out = jnp.zeros((tm, LANES), cols[0].dtype)
    for k, c in enumerate(cols):
        out = jnp.where(lane == k, c, out)
    return out


def _router_kernel(x_ref, wr_ref, br_ref, idx_ref, wts_ref, rank_ref, cnt_ref, carry_ref, *, tm, n_exp):
    @pl.when(pl.program_id(0) == 0)
    def _():
        carry_ref[...] = jnp.zeros_like(carry_ref)

    x = _unpack_bf16_pairs(x_ref[...])
    probs = _sigmoid(_dot(x, wr_ref[...]))
    choice = probs + br_ref[...]
    lane = lax.broadcasted_iota(I32, (tm, n_exp), 1)
    neg = jnp.float32(-jnp.inf)
    big = jnp.int32(n_exp)
    gsz = n_exp // N_GROUPS

    in_grp, score = [], []
    for g in range(N_GROUPS):
        msk = (lane >= g * gsz) & (lane < (g + 1) * gsz)
        v = jnp.where(msk, choice, neg)
        m1 = jnp.max(v, axis=-1, keepdims=True)
        i1 = jnp.min(jnp.where(v == m1, lane, big), axis=-1, keepdims=True)
        m2 = jnp.max(jnp.where(lane == i1, neg, v), axis=-1, keepdims=True)
        in_grp.append(msk)
        score.append(m1 + m2)

    sel_lane = jnp.zeros((tm, n_exp), I32)
    for g in range(N_GROUPS):
        beaten = jnp.zeros((tm, 1), I32)
        for o in range(N_GROUPS):
            if o == g:
                continue
            wins = (score[o] >= score[g]) if o < g else (score[o] > score[g])
            beaten = beaten + wins.astype(I32)
        sel_lane = jnp.where(in_grp[g], (beaten < TOPK_GROUPS).astype(I32), sel_lane)

    masked = jnp.where(sel_lane > 0, choice, neg)
    idx, pk = [], []
    onehot = jnp.zeros((tm, n_exp), F32)
    for _ in range(TOP_K):
        m = jnp.max(masked, axis=-1, keepdims=True)
        ik = jnp.min(jnp.where(masked == m, lane, big), axis=-1, keepdims=True)
        hit = lane == ik
        pk.append(jnp.sum(jnp.where(hit, probs, 0.0), axis=-1, keepdims=True))
        idx.append(ik)
        onehot = jnp.where(hit, 1.0, onehot)
        masked = jnp.where(hit, neg, masked)

    total = pk[0]
    for p in pk[1:]:
        total = total + p
    wts = [p / total * ROUTED_SCALE for p in pk]

    r_i = lax.broadcasted_iota(I32, (tm, tm), 0)
    c_i = lax.broadcasted_iota(I32, (tm, tm), 1)
    tri = jnp.where(c_i < r_i, 1.0, 0.0).astype(BF16)
    before = _dot(tri, onehot.astype(BF16)) + carry_ref[...]
    rank = [jnp.sum(jnp.where(lane == ik, before, 0.0), axis=-1, keepdims=True).astype(I32) for ik in idx]
    carry_ref[...] = carry_ref[...] + jnp.sum(onehot, axis=0, keepdims=True)

    idx_ref[...] = _lane_dense(idx, tm)
    wts_ref[...] = _lane_dense(wts, tm)
    rank_ref[...] = _lane_dense(rank, tm)
    cnt_ref[...] = carry_ref[...]


def _router(h2p, w_router_b, b_router):
    n, dh = h2p.shape
    n_exp = w_router_b.shape[1]
    tm = _pick(n, (320, 256, 176, 128, 64, 48, 32, 16))
    outs = pl.pallas_call(
        functools.partial(_router_kernel, tm=tm, n_exp=n_exp),
        grid=(n // tm,),
        in_specs=[
            pl.BlockSpec((tm, dh), lambda i: (i, 0)),
            pl.BlockSpec((2 * dh, n_exp), lambda i: (0, 0)),
            pl.BlockSpec((1, n_exp), lambda i: (0, 0)),
        ],
        out_specs=[
            pl.BlockSpec((tm, LANES), lambda i: (i, 0)),
            pl.BlockSpec((tm, LANES), lambda i: (i, 0)),
            pl.BlockSpec((tm, LANES), lambda i: (i, 0)),
            pl.BlockSpec((1, n_exp), lambda i: (0, 0)),
        ],
        out_shape=[
            jax.ShapeDtypeStruct((n, LANES), I32),
            jax.ShapeDtypeStruct((n, LANES), F32),
            jax.ShapeDtypeStruct((n, LANES), I32),
            jax.ShapeDtypeStruct((1, n_exp), F32),
        ],
        scratch_shapes=[pltpu.VMEM((1, n_exp), F32)],
        compiler_params=_cparams(("arbitrary",)),
    )(h2p, w_router_b, b_router.reshape(1, n_exp))
    return outs


def _dispatch_kernel(dest_ref, x_ref, wsg_ref, wsu_ref, wsd_ref, xs_hbm, sh_ref, wgb_ref, wub_ref, wdb_ref,
                     sem, *, tm):
    @pl.when(pl.program_id(0) == 0)
    def _():
        wgb_ref[...] = wsg_ref[...].astype(BF16)
        wub_ref[...] = wsu_ref[...].astype(BF16)
        wdb_ref[...] = wsd_ref[...].astype(BF16)

    def issue(t, carry):
        for k in range(TOP_K):
            d = dest_ref[0, t * TOP_K + k]
            pltpu.make_async_copy(x_ref.at[pl.ds(t, 1)], xs_hbm.at[pl.ds(d, 1)], sem).start()
        return carry

    lax.fori_loop(0, tm, issue, 0)

    x = _unpack_bf16_pairs(x_ref[...])
    act = _silu(_dot(x, wgb_ref[...])) * _dot(x, wub_ref[...])
    sh_ref[...] = _dot(act.astype(BF16), wdb_ref[...])

    for k in range(TOP_K):
        pltpu.make_async_copy(x_ref, xs_hbm.at[pl.ds(0, tm)], sem).wait()


def _dispatch_shared(h2p, dest, w_sg, w_su, w_sd, layer, n_rows):
    n, dh = h2p.shape
    d = 2 * dh
    de = w_sg.shape[2]
    tm = _pick(n, (320, 256, 176, 128, 64, 48, 32, 16))
    nt = n // tm
    xs, shared = pl.pallas_call(
        functools.partial(_dispatch_kernel, tm=tm),
        grid=(nt,),
        in_specs=[
            pl.BlockSpec((None, 1, tm * TOP_K), lambda i: (i, 0, 0), memory_space=pltpu.SMEM),
            pl.BlockSpec((tm, dh), lambda i: (i, 0)),
            pl.BlockSpec((None, d, de), lambda i: (layer, 0, 0)),
            pl.BlockSpec((None, d, de), lambda i: (layer, 0, 0)),
            pl.BlockSpec((None, de, d), lambda i: (layer, 0, 0)),
        ],
        out_specs=[
            pl.BlockSpec(memory_space=pl.ANY),
            pl.BlockSpec((tm, d), lambda i: (i, 0)),
        ],
        out_shape=[
            jax.ShapeDtypeStruct((n_rows, dh), U32),
            jax.ShapeDtypeStruct((n, d), F32),
        ],
        scratch_shapes=[
            pltpu.VMEM((d, de), BF16), pltpu.VMEM((d, de), BF16), pltpu.VMEM((de, d), BF16),
            pltpu.SemaphoreType.DMA(()),
        ],
        compiler_params=_cparams(("arbitrary",)),
    )(dest.reshape(nt, 1, tm * TOP_K), h2p, w_sg, w_su, w_sd)
    return xs, shared


def _expert_kernel(be_ref, bv_ref, na_ref, xs_ref, wg_ref, wu_ref, wd_ref, y_ref, wgb_ref, wub_ref, wdb_ref):
    i = pl.program_id(0)
    prev = be_ref[jnp.maximum(i - 1, 0)]

    @pl.when((i == 0) | (be_ref[i] != prev))
    def _():
        wgb_ref[...] = wg_ref[...].astype(BF16)
        wub_ref[...] = wu_ref[...].astype(BF16)
        wdb_ref[...] = wd_ref[...].astype(BF16)

    @pl.when(i < na_ref[0])
    def _():
        x = _unpack_bf16_pairs(xs_ref[...])
        row = lax.broadcasted_iota(I32, x.shape, 0)
        x = jnp.where(row < bv_ref[i], x, jnp.zeros_like(x))
        act = _silu(_dot(x, wgb_ref[...])) * _dot(x, wub_ref[...])
        y_ref[...] = _dot(act.astype(BF16), wdb_ref[...])


def _experts(xs, blk_exp, blk_valid, n_active, w_eg, w_eu, w_ed, layer):
    n_rows, dh = xs.shape
    d = 2 * dh
    de = w_eg.shape[3]
    nb = n_rows // ROW_BLOCK
    blk = lambda i, be, bv, na: (jnp.minimum(i, na[0] - 1), 0)
    grid_spec = pltpu.PrefetchScalarGridSpec(
        num_scalar_prefetch=3,
        grid=(nb,),
        in_specs=[
            pl.BlockSpec((ROW_BLOCK, dh), blk),
            pl.BlockSpec((None, None, d, de), lambda i, be, bv, na: (layer, be[i], 0, 0)),
            pl.BlockSpec((None, None, d, de), lambda i, be, bv, na: (layer, be[i], 0, 0)),
            pl.BlockSpec((None, None, de, d), lambda i, be, bv, na: (layer, be[i], 0, 0)),
        ],
        out_specs=pl.BlockSpec((ROW_BLOCK, d), blk),
        scratch_shapes=[pltpu.VMEM((d, de), BF16), pltpu.VMEM((d, de), BF16), pltpu.VMEM((de, d), BF16)],
    )
    return pl.pallas_call(
        _expert_kernel,
        grid_spec=grid_spec,
        out_shape=jax.ShapeDtypeStruct((n_rows, d), F32),
        compiler_params=_cparams(("arbitrary",)),
    )(blk_exp, blk_valid, n_active, xs, w_eg, w_eu, w_ed)


def _combine_kernel(dest_ref, y_hbm, wts_ref, sh_ref, x_ref, gate_ref, o_ref, gbuf, sem, *, tm):
    def issue(t, carry):
        for k in range(TOP_K):
            d = dest_ref[0, t * TOP_K + k]
            pltpu.make_async_copy(y_hbm.at[pl.ds(d, 1)], gbuf.at[k, pl.ds(t, 1)], sem).start()
        return carry

    lax.fori_loop(0, tm, issue, 0)
    for k in range(TOP_K):
        pltpu.make_async_copy(y_hbm.at[pl.ds(0, tm)], gbuf.at[k], sem).wait()

    w = wts_ref[...]
    f = sh_ref[...]
    for k in range(TOP_K):
        f = f + w[:, k:k + 1] * gbuf[k]
    o_ref[...] = x_ref[...] + gate_ref[...] * f


def _combine(y_rows, dest, wts, shared, x, gate, seq, row_off):
    m, d = x.shape
    tm = _pick(m if seq == 1 else seq, (128, 64, 32, 16))
    off = row_off // tm
    dest3 = dest[row_off:row_off + m].reshape(m // tm, 1, tm * TOP_K)
    return pl.pallas_call(
        functools.partial(_combine_kernel, tm=tm),
        grid=(m // tm,),
        in_specs=[
            pl.BlockSpec((None, 1, tm * TOP_K), lambda i: (i, 0, 0), memory_space=pltpu.SMEM),
            pl.BlockSpec(memory_space=pl.ANY),
            pl.BlockSpec((tm, LANES), lambda i: (i + off, 0)),
            pl.BlockSpec((tm, d), lambda i: (i + off, 0)),
            pl.BlockSpec((tm, d), lambda i: (i, 0)),
            _mod_spec(seq, tm, d, False),
        ],
        out_specs=pl.BlockSpec((tm, d), lambda i: (i, 0)),
        out_shape=jax.ShapeDtypeStruct((m, d), F32),
        scratch_shapes=[pltpu.VMEM((TOP_K, tm, d), F32), pltpu.SemaphoreType.DMA(())],
        compiler_params=_cparams(("arbitrary",)),
    )(dest3, y_rows, wts, shared, x, gate)


def _moe(h2p, w_router_b, b_router, w_eg, w_eu, w_ed, w_sg, w_su, w_sd, layer):
    n = h2p.shape[0]
    n_exp = w_router_b.shape[1]
    idx, wts, rank, counts = _router(h2p, w_router_b, b_router)
    idx, rank = idx[:, :TOP_K], rank[:, :TOP_K]

    sizes = counts[0].astype(I32)
    padded = (sizes + ROW_BLOCK - 1) // ROW_BLOCK * ROW_BLOCK
    pends = jnp.cumsum(padded)
    pstarts = pends - padded
    n_blocks = -(-(n * TOP_K) // ROW_BLOCK) + n_exp
    n_active = (pends[-1] // ROW_BLOCK).astype(I32)
    blk_start = jnp.arange(n_blocks, dtype=I32) * ROW_BLOCK
    blk_exp = jnp.minimum(jnp.sum(pends[None, :] <= blk_start[:, None], axis=1), n_exp - 1).astype(I32)
    blk_exp = jnp.where(jnp.arange(n_blocks) < n_active, blk_exp, blk_exp[jnp.maximum(n_active - 1, 0)])
    blk_valid = jnp.clip(pstarts[blk_exp] + sizes[blk_exp] - blk_start, 0, ROW_BLOCK).astype(I32)
    dest = (pstarts[idx] + rank).astype(I32)

    xs, shared = _dispatch_shared(h2p, dest, w_sg, w_su, w_sd, layer, n_blocks * ROW_BLOCK)
    y_rows = _experts(xs, blk_exp, blk_valid, n_active.reshape(1), w_eg, w_eu, w_ed, layer)
    return y_rows, dest, wts, shared


def _group_mods(mod_l, b0, batch, seq, d):
    m = mod_l[b0:b0 + batch].reshape(batch, N_MOD, d)
    if seq > 1:
        return [m[:, k, :].reshape(batch, 1, d) for k in range(N_MOD)]
    return [m[:, k, :].reshape(1, batch, d) for k in range(N_MOD)]


def kernel(x_prompt, x_sample, state_lru_h, state_lru_conv, state_conv, c_prompt, c_sample, w_ada, b_ada, g_mix, g_ffn, w_lru_in, w_lru_conv, b_lru_conv, w_lru_a, b_lru_a, w_lru_x, b_lru_x, lru_lambda, w_lru_out, w_pw1, b_pw1, w_dw, b_dw, g_cln, b_cln, w_pw2, b_pw2, w_router, b_router, w_exp_gate, w_exp_up, w_exp_down, w_sh_gate, w_sh_up, w_sh_down, g_final):
    bp, tp, d = x_prompt.shape
    bs, ts, _ = x_sample.shape
    assert ts == 1
    depth = w_ada.shape[0]
    d_rnn = w_lru_in.shape[2] // 2
    np_, ns = bp * tp, bs * ts

    c_all = jnp.concatenate([c_prompt, c_sample], axis=0)
    pad_rows = -c_all.shape[0] % SUBLANES
    c_all = jnp.pad(c_all, ((0, pad_rows), (0, 0)))
    mod = _ada_mod(c_all, w_ada, b_ada)

    xp = x_prompt.reshape(np_, d)
    xs = x_sample.reshape(ns, d)
    k0s = _gate_windows(d_rnn, d_rnn // N_LRU_BLOCKS)
    zero_b = jnp.zeros((d,), F32)

    hs_p, hs_s, lcs_p, lcs_s, cbs_p, cbs_s = [], [], [], [], [], []
    for l in range(depth):
        mods_p = _group_mods(mod[l], 0, bp, tp, d)
        mods_s = _group_mods(mod[l], bp, bs, ts, d)
        j = l // 2
        hp = _norm_mod(xp, g_mix[l], mods_p[0], mods_p[1], tp, False)
        hsm = _norm_mod(xs, g_mix[l], mods_s[0], mods_s[1], ts, False)
        if l % 2 == 0:
            wp = _pack_gate_weights(w_lru_a[j], w_lru_x[j])
            lru_args = (w_lru_conv[j], b_lru_conv[j], wp, b_lru_a[j], b_lru_x[j], lru_lambda[j], k0s)
            zp = _mm_plain(hp, w_lru_in, j, (768, 896, 512, 384, 256, 128))
            zs = _mm_plain(hsm, w_lru_in, j, (768, 896, 512, 384, 256, 128))
            ygp, hlp = _lru_seq(zp, bp, tp, *lru_args)
            buf = state_lru_conv[j]
            ygs, hls = _lru_step(zs, (buf[:, 0], buf[:, 1], buf[:, 2]), state_lru_h[j], *lru_args)
            hs_p.append(hlp)
            hs_s.append(hls)
            lcs_p.append(zp.reshape(bp, tp, 2 * d_rnn)[:, tp - 3:, d_rnn:])
            lcs_s.append(jnp.concatenate([buf[:, 1:], zs[:, None, d_rnn:]], axis=1))
            xp = _mm_res(ygp, w_lru_out, zero_b, xp, mods_p[2], j, tp)
            xs = _mm_res(ygs, w_lru_out, zero_b, xs, mods_s[2], j, ts)
        else:
            up = _mm_glu(hp, w_pw1, b_pw1, j)
            us = _mm_glu(hsm, w_pw1, b_pw1, j)
            conf_args = (w_dw[j], b_dw[j], g_cln[j], b_cln[j])
            vp = _conf_seq(up, bp, tp, *conf_args)
            buf = state_conv[j]
            vs = _conf_step(us, jnp.transpose(buf, (1, 0, 2)), *conf_args)
            width = w_dw.shape[1]
            cbs_p.append(up.reshape(bp, tp, d)[:, tp - (width - 1):])
            cbs_s.append(jnp.concatenate([buf[:, 1:], us[:, None, :]], axis=1))
            xp = _mm_res(vp, w_pw2, b_pw2[j], xp, mods_p[2], j, tp)
            xs = _mm_res(vs, w_pw2, b_pw2[j], xs, mods_s[2], j, ts)

        h2p = jnp.concatenate([
            _norm_mod(xp, g_ffn[l], mods_p[3], mods_p[4], tp, True),
            _norm_mod(xs, g_ffn[l], mods_s[3], mods_s[4], ts, True),
        ], axis=0)
        y_rows, dest, wts, shared = _moe(h2p, w_router[l].astype(BF16), b_router[l], w_exp_gate, w_exp_up,
                                         w_exp_down, w_sh_gate, w_sh_up, w_sh_down, l)
        xp = _combine(y_rows, dest, wts, shared, xp, mods_p[5], tp, 0)
        xs = _combine(y_rows, dest, wts, shared, xs, mods_s[5], ts, np_)

    y_prompt = _final_norm(xp, g_final).reshape(bp, tp, d)
    y_sample = _final_norm(xs, g_final).reshape(bs, ts, d)
    return (y_prompt, y_sample, jnp.stack(hs_p), jnp.stack(lcs_p), jnp.stack(cbs_p),
            jnp.stack(hs_s), jnp.stack(lcs_s), jnp.stack(cbs_s))
```

```python
import functools

import jax
import jax.numpy as jnp
from jax import lax
from jax.experimental import pallas as pl
from jax.experimental.pallas import tpu as pltpu

F32 = jnp.float32
BF16 = jnp.bfloat16
I32 = jnp.int32
U32 = jnp.uint32

N_LRU_BLOCKS = 16
LRU_C = 8.0
TOP_K = 8
N_GROUPS = 8
TOPK_GROUPS = 4
ROUTED_SCALE = 2.5
ROW_BLOCK = 128
N_MOD = 6
EPS = 1e-6

LANES = 128
SUBLANES = 8
GATE_WINDOW = 512
V7X_VMEM_BYTES = 64 * 1024 * 1024
VMEM_LIMIT = V7X_VMEM_BYTES - 8 * 1024 * 1024


def _pick(n, prefs):
    for p in prefs:
        if n % p == 0:
            return p
    return n


def _cparams(sem):
    return pltpu.CompilerParams(dimension_semantics=sem, vmem_limit_bytes=VMEM_LIMIT)


def _dot(a, b):
    return jnp.dot(a, b, preferred_element_type=F32)


def _sigmoid(x):
    return jax.nn.sigmoid(x)


def _silu(x):
    return x * jax.nn.sigmoid(x)


def _pack_bf16_pairs(h):
    half = h.shape[1] // 2
    bits = lax.bitcast_convert_type(h.astype(BF16).astype(F32), U32)
    return (bits[:, :half] >> 16) | (bits[:, half:] & jnp.uint32(0xFFFF0000))


def _unpack_bf16_pairs(w):
    lo = lax.bitcast_convert_type(w << 16, F32).astype(BF16)
    hi = lax.bitcast_convert_type(w & jnp.uint32(0xFFFF0000), F32).astype(BF16)
    return jnp.concatenate([lo, hi], axis=1)


def _ada_kernel(c_ref, w_ref, b_ref, o_ref):
    a = _silu(c_ref[...]).astype(BF16)
    o_ref[...] = _dot(a, w_ref[...].astype(BF16)) + b_ref[...]


def _ada_mod(c_all, w_ada, b_ada):
    depth, d, n = w_ada.shape
    bp = c_all.shape[0]
    tn = _pick(n, (1536, 1024, 512, 256, 128))
    return pl.pallas_call(
        _ada_kernel,
        grid=(depth, n // tn),
        in_specs=[
            pl.BlockSpec((bp, d), lambda l, j: (0, 0)),
            pl.BlockSpec((None, d, tn), lambda l, j: (l, 0, j)),
            pl.BlockSpec((None, 1, tn), lambda l, j: (l, 0, j)),
        ],
        out_specs=pl.BlockSpec((None, bp, tn), lambda l, j: (l, 0, j)),
        out_shape=jax.ShapeDtypeStruct((depth, bp, n), F32),
        compiler_params=_cparams(("arbitrary", "arbitrary")),
        name="ada_mod",
    )(c_all, w_ada, b_ada.reshape(depth, 1, n))


def _mod_spec(seq, tm, tn, col_axis_first):
    if seq > 1:
        shape = (None, 1, tn)
        if col_axis_first:
            return pl.BlockSpec(shape, lambda n, i: ((i * tm) // seq, 0, n))
        return pl.BlockSpec(shape, lambda i: ((i * tm) // seq, 0, 0))
    shape = (None, tm, tn)
    if col_axis_first:
        return pl.BlockSpec(shape, lambda n, i: (0, i, n))
    return pl.BlockSpec(shape, lambda i: (0, i, 0))


def _norm_mod_kernel(x_ref, g_ref, sh_ref, sc_ref, o_ref, *, pack):
    x = x_ref[...]
    y = x * lax.rsqrt(jnp.mean(x * x, axis=-1, keepdims=True) + EPS)
    h = (y * g_ref[...]) * (1.0 + sc_ref[...]) + sh_ref[...]
    if pack:
        o_ref[...] = _pack_bf16_pairs(h)
    else:
        o_ref[...] = h.astype(BF16)


def _norm_mod(x, g, shift, scale, seq, pack):
    m, d = x.shape
    tm = _pick(m if seq == 1 else seq, (512, 256, 128, 64, 32, 16, 8))
    out_d, out_t = (d // 2, U32) if pack else (d, BF16)
    return pl.pallas_call(
        functools.partial(_norm_mod_kernel, pack=pack),
        grid=(m // tm,),
        in_specs=[
            pl.BlockSpec((tm, d), lambda i: (i, 0)),
            pl.BlockSpec((1, d), lambda i: (0, 0)),
            _mod_spec(seq, tm, d, False),
            _mod_spec(seq, tm, d, False),
        ],
        out_specs=pl.BlockSpec((tm, out_d), lambda i: (i, 0)),
        out_shape=jax.ShapeDtypeStruct((m, out_d), out_t),
        compiler_params=_cparams(("arbitrary",)),
        name="norm_mod",
    )(x, g.reshape(1, d), shift, scale)


def _final_norm_kernel(x_ref, g_ref, o_ref):
    x = x_ref[...]
    y = x * lax.rsqrt(jnp.mean(x * x, axis=-1, keepdims=True) + EPS)
    o_ref[...] = y * g_ref[...]


def _final_norm(x, g):
    m, d = x.shape
    tm = _pick(m, (512, 256, 128, 64, 32, 16, 8))
    return pl.pallas_call(
        _final_norm_kernel,
        grid=(m // tm,),
        in_specs=[pl.BlockSpec((tm, d), lambda i: (i, 0)), pl.BlockSpec((1, d), lambda i: (0, 0))],
        out_specs=pl.BlockSpec((tm, d), lambda i: (i, 0)),
        out_shape=jax.ShapeDtypeStruct((m, d), F32),
        compiler_params=_cparams(("arbitrary",)),
        name="final_norm",
    )(x, g.reshape(1, d))


def _mm_plain_kernel(a_ref, w_ref, o_ref, wb_ref):
    @pl.when(pl.program_id(1) == 0)
    def _():
        wb_ref[...] = w_ref[...].astype(BF16)

    o_ref[...] = _dot(a_ref[...], wb_ref[...])


def _mm_plain(a, w, layer, tn_prefs):
    m, k = a.shape
    n = w.shape[2]
    tm = _pick(m, (512, 256, 128, 64, 32, 16))
    tn = _pick(n, tn_prefs)
    return pl.pallas_call(
        _mm_plain_kernel,
        grid=(n // tn, m // tm),
        in_specs=[
            pl.BlockSpec((tm, k), lambda j, i: (i, 0)),
            pl.BlockSpec((None, k, tn), lambda j, i: (layer, 0, j)),
        ],
        out_specs=pl.BlockSpec((tm, tn), lambda j, i: (i, j)),
        out_shape=jax.ShapeDtypeStruct((m, n), F32),
        scratch_shapes=[pltpu.VMEM((k, tn), BF16)],
        compiler_params=_cparams(("arbitrary", "arbitrary")),
        name="mm_plain",
    )(a, w)


def _mm_glu_kernel(a_ref, wa_ref, wg_ref, ba_ref, bg_ref, o_ref, wab_ref, wgb_ref):
    @pl.when(pl.program_id(1) == 0)
    def _():
        wab_ref[...] = wa_ref[...].astype(BF16)
        wgb_ref[...] = wg_ref[...].astype(BF16)

    a = a_ref[...]
    va = _dot(a, wab_ref[...]) + ba_ref[...]
    vg = _dot(a, wgb_ref[...]) + bg_ref[...]
    o_ref[...] = va * _sigmoid(vg)


def _mm_glu(a, w, b, layer):
    m, k = a.shape
    n = w.shape[2] // 2
    tm = _pick(m, (512, 256, 128, 64, 32, 16))
    tn = _pick(n, (512, 256, 128))
    nj = n // tn
    b3 = b.reshape(b.shape[0], 1, 2 * n)
    return pl.pallas_call(
        _mm_glu_kernel,
        grid=(nj, m // tm),
        in_specs=[
            pl.BlockSpec((tm, k), lambda j, i: (i, 0)),
            pl.BlockSpec((None, k, tn), lambda j, i: (layer, 0, j)),
            pl.BlockSpec((None, k, tn), lambda j, i: (layer, 0, j + nj)),
            pl.BlockSpec((None, 1, tn), lambda j, i: (layer, 0, j)),
            pl.BlockSpec((None, 1, tn), lambda j, i: (layer, 0, j + nj)),
        ],
        out_specs=pl.BlockSpec((tm, tn), lambda j, i: (i, j)),
        out_shape=jax.ShapeDtypeStruct((m, n), F32),
        scratch_shapes=[pltpu.VMEM((k, tn), BF16), pltpu.VMEM((k, tn), BF16)],
        compiler_params=_cparams(("arbitrary", "arbitrary")),
        name="mm_glu",
    )(a, w, w, b3, b3)


def _mm_res_kernel(a_ref, w_ref, b_ref, x_ref, gate_ref, o_ref, wb_ref):
    @pl.when(pl.program_id(1) == 0)
    def _():
        wb_ref[...] = w_ref[...].astype(BF16)

    out = _dot(a_ref[...], wb_ref[...]) + b_ref[...]
    o_ref[...] = x_ref[...] + gate_ref[...] * out


def _mm_res(a, w, b, x, gate, layer, seq):
    m, k = a.shape
    n = w.shape[2]
    tm = _pick(m if seq == 1 else seq, (512, 256, 128, 64, 32, 16))
    tn = _pick(n, (512, 256, 128))
    return pl.pallas_call(
        _mm_res_kernel,
        grid=(n // tn, m // tm),
        in_specs=[
            pl.BlockSpec((tm, k), lambda j, i: (i, 0)),
            pl.BlockSpec((None, k, tn), lambda j, i: (layer, 0, j)),
            pl.BlockSpec((1, tn), lambda j, i: (0, j)),
            pl.BlockSpec((tm, tn), lambda j, i: (i, j)),
            _mod_spec(seq, tm, tn, True),
        ],
        out_specs=pl.BlockSpec((tm, tn), lambda j, i: (i, j)),
        out_shape=jax.ShapeDtypeStruct((m, n), F32),
        scratch_shapes=[pltpu.VMEM((k, tn), BF16)],
        compiler_params=_cparams(("arbitrary", "arbitrary")),
        name="mm_res",
    )(a, w, b.reshape(1, n), x, gate)


def _gate_windows(d_rnn, bs):
    k0s = []
    for j in range(d_rnn // LANES):
        c0, c1 = LANES * j, LANES * (j + 1)
        n_lo, n_hi = c0 // bs, (c1 - 1) // bs
        k0 = min((n_lo * bs) // LANES * LANES, d_rnn - GATE_WINDOW)
        assert k0 <= n_lo * bs and (n_hi + 1) * bs <= k0 + GATE_WINDOW
        k0s.append(k0)
    return k0s


def _pack_gate_weights(w_a, w_x):
    nb, bs, _ = w_a.shape
    d_rnn = nb * bs
    eye = jnp.eye(nb, dtype=F32)

    def dense(w):
        return (w[:, :, None, :] * eye[:, None, :, None]).reshape(d_rnn, d_rnn)

    fa, fx = dense(w_a), dense(w_x)
    tiles = []
    for j, k0 in enumerate(_gate_windows(d_rnn, bs)):
        c0 = LANES * j
        tiles.append(jnp.concatenate([fa[k0:k0 + GATE_WINDOW, c0:c0 + LANES],
                                      fx[k0:k0 + GATE_WINDOW, c0:c0 + LANES]], axis=1))
    return jnp.stack(tiles).astype(BF16)


def _softplus(x):
    return jnp.maximum(x, 0.0) + jnp.log1p(jnp.exp(-jnp.abs(x)))


def _gelu_tanh(x):
    c = 0.7978845608028654
    return 0.5 * x * (1.0 + jnp.tanh(c * (x + 0.044715 * (x * x * x))))


def _lru_gate_tile(xc, xb, wp_ref, ba_ref, bx_ref, sp, j, k0):
    c0, c1 = LANES * j, LANES * (j + 1)
    pre = _dot(xb[:, k0:k0 + GATE_WINDOW], wp_ref[j])
    r = _sigmoid(pre[:, :LANES] + ba_ref[:, c0:c1])
    i = _sigmoid(pre[:, LANES:] + bx_ref[:, c0:c1])
    log_a = (-LRU_C * r) * sp[:, c0:c1]
    a = jnp.exp(log_a)
    u = jnp.sqrt(1.0 - jnp.exp(2.0 * log_a)) * (i * xc[:, c0:c1])
    return a, u


def _lru_seq_kernel(gb_ref, main_ref, wc_ref, bc_ref, wp_ref, ba_ref, bx_ref, lam_ref,
                    yg_ref, hl_ref, xp_ref, xc_ref, xb_ref, hc_ref, *, tt, k0s):
    t = pl.program_id(1)
    d_rnn = main_ref.shape[1]

    @pl.when(t == 0)
    def _():
        xp_ref[0:SUBLANES, :] = jnp.zeros((SUBLANES, d_rnn), F32)
        hc_ref[...] = jnp.zeros_like(hc_ref)

    xp_ref[SUBLANES:, :] = main_ref[...]
    acc = bc_ref[...] + wc_ref[3:4, :] * xp_ref[pl.ds(SUBLANES, tt), :]
    for k in range(3):
        acc = acc + wc_ref[k:k + 1, :] * xp_ref[pl.ds(SUBLANES - 3 + k, tt), :]
    xc_ref[...] = acc
    xb_ref[...] = acc.astype(BF16)
    xp_ref[0:SUBLANES, :] = xp_ref[tt:tt + SUBLANES, :]

    sp = _softplus(-lam_ref[...])
    row = lax.broadcasted_iota(I32, (tt, LANES), 0)
    for j, k0 in enumerate(k0s):
        c0, c1 = LANES * j, LANES * (j + 1)
        a, u = _lru_gate_tile(xc_ref, xb_ref, wp_ref, ba_ref, bx_ref, sp, j, k0)
        d = 1
        while d < tt:
            keep = row >= d
            a_s = jnp.where(keep, pltpu.roll(a, d, 0), 1.0)
            u_s = jnp.where(keep, pltpu.roll(u, d, 0), 0.0)
            u = a * u_s + u
            a = a * a_s
            d *= 2
        h = a * hc_ref[:, c0:c1] + u
        hc_ref[:, c0:c1] = h[tt - 1:tt, :]
        yg_ref[:, c0:c1] = (h * _gelu_tanh(gb_ref[:, c0:c1])).astype(BF16)

    @pl.when(t == pl.num_programs(1) - 1)
    def _():
        hl_ref[...] = hc_ref[...]


def _lru_seq(z, batch, seq, w_conv, b_conv, wp, b_a, b_x, lam, k0s):
    d_rnn = z.shape[1] // 2
    tt = _pick(seq, (256, 128, 64, 32, 16, 8))
    nt = seq // tt
    row = lambda b, t: b * nt + t
    full = lambda shape: pl.BlockSpec(shape, lambda b, t: (0,) * len(shape))
    yg, h_last = pl.pallas_call(
        functools.partial(_lru_seq_kernel, tt=tt, k0s=k0s),
        grid=(batch, nt),
        in_specs=[
            pl.BlockSpec((tt, d_rnn), lambda b, t: (row(b, t), 0)),
            pl.BlockSpec((tt, d_rnn), lambda b, t: (row(b, t), 1)),
            full((4, d_rnn)), full((1, d_rnn)), full(wp.shape),
            full((1, d_rnn)), full((1, d_rnn)), full((1, d_rnn)),
        ],
        out_specs=[
            pl.BlockSpec((tt, d_rnn), lambda b, t: (row(b, t), 0)),
            pl.BlockSpec((None, 1, d_rnn), lambda b, t: (b, 0, 0)),
        ],
        out_shape=[
            jax.ShapeDtypeStruct((batch * seq, d_rnn), BF16),
            jax.ShapeDtypeStruct((batch, 1, d_rnn), F32),
        ],
        scratch_shapes=[
            pltpu.VMEM((tt + SUBLANES, d_rnn), F32),
            pltpu.VMEM((tt, d_rnn), F32),
            pltpu.VMEM((tt, d_rnn), BF16),
            pltpu.VMEM((1, d_rnn), F32),
        ],
        compiler_params=_cparams(("arbitrary", "arbitrary")),
        name="lru_seq",
    )(z, z, w_conv, b_conv.reshape(1, -1), wp, b_a.reshape(1, -1), b_x.reshape(1, -1), lam.reshape(1, -1))
    return yg, h_last.reshape(batch, d_rnn)


def _lru_step_kernel(gb_ref, main_ref, p0_ref, p1_ref, p2_ref, h0_ref, wc_ref, bc_ref, wp_ref, ba_ref,
                     bx_ref, lam_ref, yg_ref, h_ref, xc_ref, xb_ref, *, k0s):
    acc = (bc_ref[...] + wc_ref[3:4, :] * main_ref[...] + wc_ref[2:3, :] * p2_ref[...]
           + wc_ref[1:2, :] * p1_ref[...] + wc_ref[0:1, :] * p0_ref[...])
    xc_ref[...] = acc
    xb_ref[...] = acc.astype(BF16)
    sp = _softplus(-lam_ref[...])
    for j, k0 in enumerate(k0s):
        c0, c1 = LANES * j, LANES * (j + 1)
        a, u = _lru_gate_tile(xc_ref, xb_ref, wp_ref, ba_ref, bx_ref, sp, j, k0)
        h = a * h0_ref[:, c0:c1] + u
        h_ref[:, c0:c1] = h
        yg_ref[:, c0:c1] = (h * _gelu_tanh(gb_ref[:, c0:c1])).astype(BF16)


def _lru_step(z, prev, h0, w_conv, b_conv, wp, b_a, b_x, lam, k0s):
    m = z.shape[0]
    d_rnn = z.shape[1] // 2
    tm = _pick(m, (128, 64, 32, 16))
    rows = lambda c: pl.BlockSpec((tm, d_rnn), lambda i: (i, c))
    full = lambda shape: pl.BlockSpec(shape, lambda i: (0,) * len(shape))
    return pl.pallas_call(
        functools.partial(_lru_step_kernel, k0s=k0s),
        grid=(m // tm,),
        in_specs=[
            rows(0), rows(1), rows(0), rows(0), rows(0), rows(0),
            full((4, d_rnn)), full((1, d_rnn)), full(wp.shape),
            full((1, d_rnn)), full((1, d_rnn)), full((1, d_rnn)),
        ],
        out_specs=[rows(0), rows(0)],
        out_shape=[jax.ShapeDtypeStruct((m, d_rnn), BF16), jax.ShapeDtypeStruct((m, d_rnn), F32)],
        scratch_shapes=[pltpu.VMEM((tm, d_rnn), F32), pltpu.VMEM((tm, d_rnn), BF16)],
        compiler_params=_cparams(("arbitrary",)),
        name="lru_step",
    )(z, z, prev[0], prev[1], prev[2], h0, w_conv, b_conv.reshape(1, -1), wp,
      b_a.reshape(1, -1), b_x.reshape(1, -1), lam.reshape(1, -1))


def _ln_silu(v, g, b):
    mu = jnp.mean(v, axis=-1, keepdims=True)
    dv = v - mu
    var = jnp.mean(dv * dv, axis=-1, keepdims=True)
    return _silu(dv * lax.rsqrt(var + EPS) * g + b)


def _conf_seq_kernel(u_ref, w_ref, b_ref, g_ref, bl_ref, o_ref, xp_ref, v_ref, *, tt, width, pad):
    d = u_ref.shape[1]

    @pl.when(pl.program_id(1) == 0)
    def _():
        xp_ref[0:pad, :] = jnp.zeros((pad, d), F32)

    xp_ref[pad:, :] = u_ref[...]
    base = pad - (width - 1)
    for c in range(d // LANES):
        c0, c1 = LANES * c, LANES * (c + 1)
        acc = b_ref[:, c0:c1] + w_ref[0:1, c0:c1] * xp_ref[pl.ds(base, tt), c0:c1]
        for k in range(1, width):
            acc = acc + w_ref[k:k + 1, c0:c1] * xp_ref[pl.ds(base + k, tt), c0:c1]
        v_ref[:, c0:c1] = acc
    xp_ref[0:pad, :] = xp_ref[tt:tt + pad, :]
    o_ref[...] = _ln_silu(v_ref[...], g_ref[...], bl_ref[...]).astype(BF16)


def _conf_seq(u, batch, seq, w_dw, b_dw, g_ln, b_ln):
    d = u.shape[1]
    width = w_dw.shape[0]
    pad = -(-(width - 1) // SUBLANES) * SUBLANES
    tt = _pick(seq, (256, 128, 64, 32))
    nt = seq // tt
    full = lambda shape: pl.BlockSpec(shape, lambda b, t: (0,) * len(shape))
    return pl.pallas_call(
        functools.partial(_conf_seq_kernel, tt=tt, width=width, pad=pad),
        grid=(batch, nt),
        in_specs=[
            pl.BlockSpec((tt, d), lambda b, t: (b * nt + t, 0)),
            full((width, d)), full((1, d)), full((1, d)), full((1, d)),
        ],
        out_specs=pl.BlockSpec((tt, d), lambda b, t: (b * nt + t, 0)),
        out_shape=jax.ShapeDtypeStruct((batch * seq, d), BF16),
        scratch_shapes=[pltpu.VMEM((tt + pad, d), F32), pltpu.VMEM((tt, d), F32)],
        compiler_params=_cparams(("arbitrary", "arbitrary")),
        name="conf_seq",
    )(u, w_dw, b_dw.reshape(1, d), g_ln.reshape(1, d), b_ln.reshape(1, d))


def _conf_step_kernel(u_ref, buf_ref, w_ref, b_ref, g_ref, bl_ref, o_ref, *, width):
    acc = b_ref[...] + w_ref[width - 1:width, :] * u_ref[...]
    for k in range(width - 1):
        acc = acc + w_ref[k:k + 1, :] * buf_ref[k]
    o_ref[...] = _ln_silu(acc, g_ref[...], bl_ref[...]).astype(BF16)


def _conf_step(u, buf_t, w_dw, b_dw, g_ln, b_ln):
    m, d = u.shape
    width = w_dw.shape[0]
    tm = _pick(m, (32, 16, 8))
    full = lambda shape: pl.BlockSpec(shape, lambda i: (0,) * len(shape))
    return pl.pallas_call(
        functools.partial(_conf_step_kernel, width=width),
        grid=(m // tm,),
        in_specs=[
            pl.BlockSpec((tm, d), lambda i: (i, 0)),
            pl.BlockSpec((width - 1, tm, d), lambda i: (0, i, 0)),
            full((width, d)), full((1, d)), full((1, d)), full((1, d)),
        ],
        out_specs=pl.BlockSpec((tm, d), lambda i: (i, 0)),
        out_shape=jax.ShapeDtypeStruct((m, d), BF16),
        compiler_params=_cparams(("arbitrary",)),
        name="conf_step",
    )(u, buf_t, w_dw, b_dw.reshape(1, d), g_ln.reshape(1, d), b_ln.reshape(1, d))


def _lane_dense(cols, tm):
    lane = lax.broadcasted_iota(I32, (tm, LANES), 1)
    out = jnp.zeros((tm, LANES), cols[0].dtype)
    for k, c in enumerate(cols):
        out = jnp.where(lane == k, c, out)
    return out


def _router_kernel(x_ref, wr_ref, br_ref, idx_ref, wts_ref, rank_ref, cnt_ref, carry_ref, *, tm, n_exp):
    @pl.when(pl.program_id(0) == 0)
    def _():
        carry_ref[...] = jnp.zeros_like(carry_ref)

    x = _unpack_bf16_pairs(x_ref[...])
    probs = _sigmoid(_dot(x, wr_ref[...]))
    choice = probs + br_ref[...]
    lane = lax.broadcasted_iota(I32, (tm, n_exp), 1)
    neg = jnp.float32(-jnp.inf)
    big = jnp.int32(n_exp)
    gsz = n_exp // N_GROUPS

    in_grp, score = [], []
    for g in range(N_GROUPS):
        msk = (lane >= g * gsz) & (lane < (g + 1) * gsz)
        v = jnp.where(msk, choice, neg)
        m1 = jnp.max(v, axis=-1, keepdims=True)
        i1 = jnp.min(jnp.where(v == m1, lane, big), axis=-1, keepdims=True)
        m2 = jnp.max(jnp.where(lane == i1, neg, v), axis=-1, keepdims=True)
        in_grp.append(msk)
        score.append(m1 + m2)

    sel_lane = jnp.zeros((tm, n_exp), I32)
    for g in range(N_GROUPS):
        beaten = jnp.zeros((tm, 1), I32)
        for o in range(N_GROUPS):
            if o == g:
                continue
            wins = (score[o] >= score[g]) if o < g else (score[o] > score[g])
            beaten = beaten + wins.astype(I32)
        sel_lane = jnp.where(in_grp[g], (beaten < TOPK_GROUPS).astype(I32), sel_lane)

    masked = jnp.where(sel_lane > 0, choice, neg)
    idx, pk = [], []
    onehot = jnp.zeros((tm, n_exp), F32)
    for _ in range(TOP_K):
        m = jnp.max(masked, axis=-1, keepdims=True)
        ik = jnp.min(jnp.where(masked == m, lane, big), axis=-1, keepdims=True)
        hit = lane == ik
        pk.append(jnp.sum(jnp.where(hit, probs, 0.0), axis=-1, keepdims=True))
        idx.append(ik)
        onehot = jnp.where(hit, 1.0, onehot)
        masked = jnp.where(hit, neg, masked)

    total = pk[0]
    for p in pk[1:]:
        total = total + p
    wts = [p / total * ROUTED_SCALE for p in pk]

    r_i = lax.broadcasted_iota(I32, (tm, tm), 0)
    c_i = lax.broadcasted_iota(I32, (tm, tm), 1)
    tri = jnp.where(c_i < r_i, 1.0, 0.0).astype(BF16)
    before = _dot(tri, onehot.astype(BF16)) + carry_ref[...]
    rank = [jnp.sum(jnp.where(lane == ik, before, 0.0), axis=-1, keepdims=True).astype(I32) for ik in idx]
    carry_ref[...] = carry_ref[...] + jnp.sum(onehot, axis=0, keepdims=True)

    idx_ref[...] = _lane_dense(idx, tm)
    wts_ref[...] = _lane_dense(wts, tm)
    rank_ref[...] = _lane_dense(rank, tm)
    cnt_ref[...] = carry_ref[...]


def _router(h2p, w_router_b, b_router):
    n, dh = h2p.shape
    n_exp = w_router_b.shape[1]
    tm = _pick(n, (320, 256, 176, 128, 64, 48, 32, 16))
    outs = pl.pallas_call(
        functools.partial(_router_kernel, tm=tm, n_exp=n_exp),
        grid=(n // tm,),
        in_specs=[
            pl.BlockSpec((tm, dh), lambda i: (i, 0)),
            pl.BlockSpec((2 * dh, n_exp), lambda i: (0, 0)),
            pl.BlockSpec((1, n_exp), lambda i: (0, 0)),
        ],
        out_specs=[
            pl.BlockSpec((tm, LANES), lambda i: (i, 0)),
            pl.BlockSpec((tm, LANES), lambda i: (i, 0)),
            pl.BlockSpec((tm, LANES), lambda i: (i, 0)),
            pl.BlockSpec((1, n_exp), lambda i: (0, 0)),
        ],
        out_shape=[
            jax.ShapeDtypeStruct((n, LANES), I32),
            jax.ShapeDtypeStruct((n, LANES), F32),
            jax.ShapeDtypeStruct((n, LANES), I32),
            jax.ShapeDtypeStruct((1, n_exp), F32),
        ],
        scratch_shapes=[pltpu.VMEM((1, n_exp), F32)],
        compiler_params=_cparams(("arbitrary",)),
        name="moe_router",
    )(h2p, w_router_b, b_router.reshape(1, n_exp))
    return outs


def _dest_kernel(idx_ref, rank_ref, ps_ref, o_ref, *, tm, n_exp):
    lane = lax.broadcasted_iota(I32, (tm, n_exp), 1)
    idx = idx_ref[...]
    start = []
    for k in range(TOP_K):
        hit = lane == idx[:, k:k + 1]
        start.append(jnp.sum(jnp.where(hit, ps_ref[...], 0.0), axis=-1, keepdims=True).astype(I32))
    o_ref[...] = _lane_dense(start, tm) + rank_ref[...]


def _dest_rows(idx, rank, pstarts):
    n = idx.shape[0]
    n_exp = pstarts.shape[1]
    tm = _pick(n, (320, 256, 176, 128, 64, 48, 32, 16))
    return pl.pallas_call(
        functools.partial(_dest_kernel, tm=tm, n_exp=n_exp),
        grid=(n // tm,),
        in_specs=[
            pl.BlockSpec((tm, LANES), lambda i: (i, 0)),
            pl.BlockSpec((tm, LANES), lambda i: (i, 0)),
            pl.BlockSpec((1, n_exp), lambda i: (0, 0)),
        ],
        out_specs=pl.BlockSpec((tm, LANES), lambda i: (i, 0)),
        out_shape=jax.ShapeDtypeStruct((n, LANES), I32),
        compiler_params=_cparams(("arbitrary",)),
        name="moe_dest",
    )(idx, rank, pstarts)


def _dispatch_kernel(dest_ref, x_ref, wsg_ref, wsu_ref, wsd_ref, xs_hbm, sh_ref, wgb_ref, wub_ref, wdb_ref,
                     sem, *, tm):
    @pl.when(pl.program_id(0) == 0)
    def _():
        wgb_ref[...] = wsg_ref[...].astype(BF16)
        wub_ref[...] = wsu_ref[...].astype(BF16)
        wdb_ref[...] = wsd_ref[...].astype(BF16)

    def issue(t, carry):
        for k in range(TOP_K):
            d = dest_ref[0, t * TOP_K + k]
            pltpu.make_async_copy(x_ref.at[pl.ds(t, 1)], xs_hbm.at[pl.ds(d, 1)], sem).start()
        return carry

    lax.fori_loop(0, tm, issue, 0)

    x = _unpack_bf16_pairs(x_ref[...])
    act = _silu(_dot(x, wgb_ref[...])) * _dot(x, wub_ref[...])
    sh_ref[...] = _dot(act.astype(BF16), wdb_ref[...])

    for k in range(TOP_K):
        pltpu.make_async_copy(x_ref, xs_hbm.at[pl.ds(0, tm)], sem).wait()


def _dispatch_shared(h2p, dest, w_sg, w_su, w_sd, layer, n_rows):
    n, dh = h2p.shape
    d = 2 * dh
    de = w_sg.shape[2]
    tm = _pick(n, (320, 256, 176, 128, 64, 48, 32, 16))
    nt = n // tm
    xs, shared = pl.pallas_call(
        functools.partial(_dispatch_kernel, tm=tm),
        grid=(nt,),
        in_specs=[
            pl.BlockSpec((None, 1, tm * TOP_K), lambda i: (i, 0, 0), memory_space=pltpu.SMEM),
            pl.BlockSpec((tm, dh), lambda i: (i, 0)),
            pl.BlockSpec((None, d, de), lambda i: (layer, 0, 0)),
            pl.BlockSpec((None, d, de), lambda i: (layer, 0, 0)),
            pl.BlockSpec((None, de, d), lambda i: (layer, 0, 0)),
        ],
        out_specs=[
            pl.BlockSpec(memory_space=pl.ANY),
            pl.BlockSpec((tm, d), lambda i: (i, 0)),
        ],
        out_shape=[
            jax.ShapeDtypeStruct((n_rows, dh), U32),
            jax.ShapeDtypeStruct((n, d), F32),
        ],
        scratch_shapes=[
            pltpu.VMEM((d, de), BF16), pltpu.VMEM((d, de), BF16), pltpu.VMEM((de, d), BF16),
            pltpu.SemaphoreType.DMA(()),
        ],
        compiler_params=_cparams(("arbitrary",)),
        name="moe_dispatch",
    )(dest.reshape(nt, 1, tm * TOP_K), h2p, w_sg, w_su, w_sd)
    return xs, shared


def _expert_kernel(be_ref, bv_ref, na_ref, sz_ref, xs_ref, wg_hbm, wu_hbm, wd_hbm, y_ref,
                   wg_buf, wu_buf, wd_buf, wgb_ref, wub_ref, wdb_ref, sem, slot_ref, *, layer, n_exp):
    i = pl.program_id(0)
    e = be_ref[i]

    def weight_copies(expert, slot):
        return [
            pltpu.make_async_copy(wg_hbm.at[layer, expert], wg_buf.at[slot], sem.at[slot, 0]),
            pltpu.make_async_copy(wu_hbm.at[layer, expert], wu_buf.at[slot], sem.at[slot, 1]),
            pltpu.make_async_copy(wd_hbm.at[layer, expert], wd_buf.at[slot], sem.at[slot, 2]),
        ]

    @pl.when(i == 0)
    def _():
        slot_ref[0] = 0
        for c in weight_copies(e, 0):
            c.start()

    is_first = (i == 0) | (e != be_ref[jnp.maximum(i - 1, 0)])

    @pl.when(is_first & (i < na_ref[0]))
    def _():
        slot = slot_ref[0]
        nxt = lax.while_loop(lambda n: (n < n_exp) & (sz_ref[jnp.minimum(n, n_exp - 1)] == 0),
                             lambda n: n + 1, e + 1)

        @pl.when(nxt < n_exp)
        def _():
            for c in weight_copies(nxt, 1 - slot):
                c.start()

        for c in weight_copies(e, slot):
            c.wait()
        wgb_ref[...] = wg_buf[slot].astype(BF16)
        wub_ref[...] = wu_buf[slot].astype(BF16)
        wdb_ref[...] = wd_buf[slot].astype(BF16)
        slot_ref[0] = 1 - slot

    @pl.when(i < na_ref[0])
    def _():
        x = _unpack_bf16_pairs(xs_ref[...])
        row = lax.broadcasted_iota(I32, x.shape, 0)
        x = jnp.where(row < bv_ref[i], x, jnp.zeros_like(x))
        act = _silu(_dot(x, wgb_ref[...])) * _dot(x, wub_ref[...])
        y_ref[...] = _dot(act.astype(BF16), wdb_ref[...])


def _experts(xs, blk_exp, blk_valid, n_active, sizes, w_eg, w_eu, w_ed, layer):
    n_rows, dh = xs.shape
    d = 2 * dh
    n_exp, _, de = w_eg.shape[1:]
    nb = n_rows // ROW_BLOCK
    blk = lambda i, be, bv, na, sz: (jnp.minimum(i, na[0] - 1), 0)
    grid_spec = pltpu.PrefetchScalarGridSpec(
        num_scalar_prefetch=4,
        grid=(nb,),
        in_specs=[
            pl.BlockSpec((ROW_BLOCK, dh), blk),
            pl.BlockSpec(memory_space=pl.ANY),
            pl.BlockSpec(memory_space=pl.ANY),
            pl.BlockSpec(memory_space=pl.ANY),
        ],
        out_specs=pl.BlockSpec((ROW_BLOCK, d), blk),
        scratch_shapes=[
            pltpu.VMEM((2, d, de), F32), pltpu.VMEM((2, d, de), F32), pltpu.VMEM((2, de, d), F32),
            pltpu.VMEM((d, de), BF16), pltpu.VMEM((d, de), BF16), pltpu.VMEM((de, d), BF16),
            pltpu.SemaphoreType.DMA((2, 3)),
            pltpu.SMEM((1,), I32),
        ],
    )
    return pl.pallas_call(
        functools.partial(_expert_kernel, layer=layer, n_exp=n_exp),
        grid_spec=grid_spec,
        out_shape=jax.ShapeDtypeStruct((n_rows, d), F32),
        compiler_params=_cparams(("arbitrary",)),
        name="moe_experts",
    )(blk_exp, blk_valid, n_active, sizes, xs, w_eg, w_eu, w_ed)


def _combine_kernel(dest_ref, dnext_ref, y_hbm, wts_ref, sh_ref, x_ref, gate_ref, o_ref, gbuf, sem, *, tm):
    i = pl.program_id(0)
    slot = lax.rem(i, 2)

    def issue(dref, s):
        def body(t, carry):
            for k in range(TOP_K):
                d = dref[0, t * TOP_K + k]
                pltpu.make_async_copy(y_hbm.at[pl.ds(d, 1)], gbuf.at[s, k, pl.ds(t, 1)], sem.at[s]).start()
            return carry

        lax.fori_loop(0, tm, body, 0)

    @pl.when(i == 0)
    def _():
        issue(dest_ref, 0)

    @pl.when(i + 1 < pl.num_programs(0))
    def _():
        issue(dnext_ref, 1 - slot)

    for k in range(TOP_K):
        pltpu.make_async_copy(y_hbm.at[pl.ds(0, tm)], gbuf.at[slot, k], sem.at[slot]).wait()

    w = wts_ref[...]
    f = sh_ref[...]
    for k in range(TOP_K):
        f = f + w[:, k:k + 1] * gbuf[slot, k]
    o_ref[...] = x_ref[...] + gate_ref[...] * f


def _combine(y_rows, dest, wts, shared, x, gate, seq, row_off):
    m, d = x.shape
    tm = _pick(m if seq == 1 else seq, (128, 64, 32, 16))
    off = row_off // tm
    nt = m // tm
    dest3 = dest[row_off:row_off + m].reshape(nt, 1, tm * TOP_K)
    return pl.pallas_call(
        functools.partial(_combine_kernel, tm=tm),
        grid=(nt,),
        in_specs=[
            pl.BlockSpec((None, 1, tm * TOP_K), lambda i: (i, 0, 0), memory_space=pltpu.SMEM),
            pl.BlockSpec((None, 1, tm * TOP_K), lambda i: (jnp.minimum(i + 1, nt - 1), 0, 0),
                         memory_space=pltpu.SMEM),
            pl.BlockSpec(memory_space=pl.ANY),
            pl.BlockSpec((tm, LANES), lambda i: (i + off, 0)),
            pl.BlockSpec((tm, d), lambda i: (i + off, 0)),
            pl.BlockSpec((tm, d), lambda i: (i, 0)),
            _mod_spec(seq, tm, d, False),
        ],
        out_specs=pl.BlockSpec((tm, d), lambda i: (i, 0)),
        out_shape=jax.ShapeDtypeStruct((m, d), F32),
        scratch_shapes=[pltpu.VMEM((2, TOP_K, tm, d), F32), pltpu.SemaphoreType.DMA((2,))],
        compiler_params=_cparams(("arbitrary",)),
        name="moe_combine",
    )(dest3, dest3, y_rows, wts, shared, x, gate)


def _moe(h2p, w_router_b, b_router, w_eg, w_eu, w_ed, w_sg, w_su, w_sd, layer):
    n = h2p.shape[0]
    n_exp = w_router_b.shape[1]
    idx, wts, rank, counts = _router(h2p, w_router_b, b_router)

    sizes = counts[0].astype(I32)
    padded = (sizes + ROW_BLOCK - 1) // ROW_BLOCK * ROW_BLOCK
    pends = jnp.cumsum(padded)
    pstarts = pends - padded
    n_blocks = -(-(n * TOP_K) // ROW_BLOCK) + n_exp
    n_active = (pends[-1] // ROW_BLOCK).astype(I32)
    experts = jnp.arange(n_exp, dtype=I32)
    last_exp = jnp.max(jnp.where(sizes > 0, experts, 0))
    blk_start = jnp.arange(n_blocks, dtype=I32) * ROW_BLOCK
    blk_exp = jnp.minimum(jnp.sum(pends[None, :] <= blk_start[:, None], axis=1).astype(I32), last_exp)
    seg_end = jnp.sum(jnp.where(blk_exp[:, None] == experts[None, :], (pstarts + sizes)[None, :], 0), axis=1)
    blk_valid = jnp.clip(seg_end - blk_start, 0, ROW_BLOCK).astype(I32)
    dest = _dest_rows(idx, rank, pstarts.astype(F32).reshape(1, n_exp))[:, :TOP_K]

    xs, shared = _dispatch_shared(h2p, dest, w_sg, w_su, w_sd, layer, n_blocks * ROW_BLOCK)
    y_rows = _experts(xs, blk_exp, blk_valid, n_active.reshape(1), sizes, w_eg, w_eu, w_ed, layer)
    return y_rows, dest, wts, shared


def _group_mods(mod_l, b0, batch, seq, d):
    m = mod_l[b0:b0 + batch].reshape(batch, N_MOD, d)
    if seq > 1:
        return [m[:, k, :].reshape(batch, 1, d) for k in range(N_MOD)]
    return [m[:, k, :].reshape(1, batch, d) for k in range(N_MOD)]


def kernel(x_prompt, x_sample, state_lru_h, state_lru_conv, state_conv, c_prompt, c_sample, w_ada, b_ada, g_mix, g_ffn, w_lru_in, w_lru_conv, b_lru_conv, w_lru_a, b_lru_a, w_lru_x, b_lru_x, lru_lambda, w_lru_out, w_pw1, b_pw1, w_dw, b_dw, g_cln, b_cln, w_pw2, b_pw2, w_router, b_router, w_exp_gate, w_exp_up, w_exp_down, w_sh_gate, w_sh_up, w_sh_down, g_final):
    bp, tp, d = x_prompt.shape
    bs, ts, _ = x_sample.shape
    assert ts == 1
    depth = w_ada.shape[0]
    d_rnn = w_lru_in.shape[2] // 2
    np_, ns = bp * tp, bs * ts

    c_all = jnp.concatenate([c_prompt, c_sample], axis=0)
    pad_rows = -c_all.shape[0] % SUBLANES
    c_all = jnp.pad(c_all, ((0, pad_rows), (0, 0)))
    mod = _ada_mod(c_all, w_ada, b_ada)

    xp = x_prompt.reshape(np_, d)
    xs = x_sample.reshape(ns, d)
    k0s = _gate_windows(d_rnn, d_rnn // N_LRU_BLOCKS)
    zero_b = jnp.zeros((d,), F32)

    hs_p, hs_s, lcs_p, lcs_s, cbs_p, cbs_s = [], [], [], [], [], []
    for l in range(depth):
        mods_p = _group_mods(mod[l], 0, bp, tp, d)
        mods_s = _group_mods(mod[l], bp, bs, ts, d)
        j = l // 2
        hp = _norm_mod(xp, g_mix[l], mods_p[0], mods_p[1], tp, False)
        hsm = _norm_mod(xs, g_mix[l], mods_s[0], mods_s[1], ts, False)
        if l % 2 == 0:
            wp = _pack_gate_weights(w_lru_a[j], w_lru_x[j])
            lru_args = (w_lru_conv[j], b_lru_conv[j], wp, b_lru_a[j], b_lru_x[j], lru_lambda[j], k0s)
            zp = _mm_plain(hp, w_lru_in, j, (768, 896, 512, 384, 256, 128))
            zs = _mm_plain(hsm, w_lru_in, j, (768, 896, 512, 384, 256, 128))
            ygp, hlp = _lru_seq(zp, bp, tp, *lru_args)
            buf = state_lru_conv[j]
            ygs, hls = _lru_step(zs, (buf[:, 0], buf[:, 1], buf[:, 2]), state_lru_h[j], *lru_args)
            hs_p.append(hlp)
            hs_s.append(hls)
            lcs_p.append(zp.reshape(bp, tp, 2 * d_rnn)[:, tp - 3:, d_rnn:])
            lcs_s.append(jnp.concatenate([buf[:, 1:], zs[:, None, d_rnn:]], axis=1))
            xp = _mm_res(ygp, w_lru_out, zero_b, xp, mods_p[2], j, tp)
            xs = _mm_res(ygs, w_lru_out, zero_b, xs, mods_s[2], j, ts)
        else:
            up = _mm_glu(hp, w_pw1, b_pw1, j)
            us = _mm_glu(hsm, w_pw1, b_pw1, j)
            conf_args = (w_dw[j], b_dw[j], g_cln[j], b_cln[j])
            vp = _conf_seq(up, bp, tp, *conf_args)
            buf = state_conv[j]
            vs = _conf_step(us, jnp.transpose(buf, (1, 0, 2)), *conf_args)
            width = w_dw.shape[1]
            cbs_p.append(up.reshape(bp, tp, d)[:, tp - (width - 1):])
            cbs_s.append(jnp.concatenate([buf[:, 1:], us[:, None, :]], axis=1))
            xp = _mm_res(vp, w_pw2, b_pw2[j], xp, mods_p[2], j, tp)
            xs = _mm_res(vs, w_pw2, b_pw2[j], xs, mods_s[2], j, ts)

        h2p = jnp.concatenate([
            _norm_mod(xp, g_ffn[l], mods_p[3], mods_p[4], tp, True),
            _norm_mod(xs, g_ffn[l], mods_s[3], mods_s[4], ts, True),
        ], axis=0)
        y_rows, dest, wts, shared = _moe(h2p, w_router[l].astype(BF16), b_router[l], w_exp_gate, w_exp_up,
                                         w_exp_down, w_sh_gate, w_sh_up, w_sh_down, l)
        xp = _combine(y_rows, dest, wts, shared, xp, mods_p[5], tp, 0)
        xs = _combine(y_rows, dest, wts, shared, xs, mods_s[5], ts, np_)

    y_prompt = _final_norm(xp, g_final).reshape(bp, tp, d)
    y_sample = _final_norm(xs, g_final).reshape(bs, ts, d)
    return (y_prompt, y_sample, jnp.stack(hs_p), jnp.stack(lcs_p), jnp.stack(cbs_p),
            jnp.stack(hs_s), jnp.stack(lcs_s), jnp.stack(cbs_s))
```

```python
import functools

import jax
import jax.numpy as jnp
from jax import lax
from jax.experimental import pallas as pl
from jax.experimental.pallas import tpu as pltpu

F32 = jnp.float32
BF16 = jnp.bfloat16
I32 = jnp.int32
U32 = jnp.uint32

N_LRU_BLOCKS = 16
LRU_C = 8.0
TOP_K = 8
N_GROUPS = 8
TOPK_GROUPS = 4
ROUTED_SCALE = 2.5
ROW_BLOCK = 128
N_MOD = 6
EPS = 1e-6

LANES = 128
SUBLANES = 8
GATE_WINDOW = 512
V7X_VMEM_BYTES = 64 * 1024 * 1024
VMEM_LIMIT = V7X_VMEM_BYTES - 8 * 1024 * 1024


def _pick(n, prefs):
    for p in prefs:
        if n % p == 0:
            return p
    return n


def _cparams(sem):
    return pltpu.CompilerParams(dimension_semantics=sem, vmem_limit_bytes=VMEM_LIMIT)


def _dot(a, b):
    return jnp.dot(a, b, preferred_element_type=F32)


def _sigmoid(x):
    return jax.nn.sigmoid(x)


def _silu(x):
    return x * jax.nn.sigmoid(x)


def _pack_bf16_pairs(h):
    half = h.shape[1] // 2
    bits = lax.bitcast_convert_type(h.astype(BF16).astype(F32), U32)
    return (bits[:, :half] >> 16) | (bits[:, half:] & jnp.uint32(0xFFFF0000))


def _unpack_bf16_pairs(w):
    lo = lax.bitcast_convert_type(w << 16, F32).astype(BF16)
    hi = lax.bitcast_convert_type(w & jnp.uint32(0xFFFF0000), F32).astype(BF16)
    return jnp.concatenate([lo, hi], axis=1)


def _ada_kernel(c_ref, w_ref, b_ref, o_ref):
    a = _silu(c_ref[...]).astype(BF16)
    o_ref[...] = _dot(a, w_ref[...].astype(BF16)) + b_ref[...]


def _ada_mod(c_all, w_ada, b_ada):
    depth, d, n = w_ada.shape
    bp = c_all.shape[0]
    tn = _pick(n, (1536, 1024, 512, 256, 128))
    return pl.pallas_call(
        _ada_kernel,
        grid=(depth, n // tn),
        in_specs=[
            pl.BlockSpec((bp, d), lambda l, j: (0, 0)),
            pl.BlockSpec((None, d, tn), lambda l, j: (l, 0, j)),
            pl.BlockSpec((None, 1, tn), lambda l, j: (l, 0, j)),
        ],
        out_specs=pl.BlockSpec((None, bp, tn), lambda l, j: (l, 0, j)),
        out_shape=jax.ShapeDtypeStruct((depth, bp, n), F32),
        compiler_params=_cparams(("arbitrary", "arbitrary")),
        name="ada_mod",
    )(c_all, w_ada, b_ada.reshape(depth, 1, n))


def _mod_spec(seq, tm, tn, col_axis_first):
    if seq > 1:
        shape = (None, 1, tn)
        if col_axis_first:
            return pl.BlockSpec(shape, lambda n, i: ((i * tm) // seq, 0, n))
        return pl.BlockSpec(shape, lambda i: ((i * tm) // seq, 0, 0))
    shape = (None, tm, tn)
    if col_axis_first:
        return pl.BlockSpec(shape, lambda n, i: (0, i, n))
    return pl.BlockSpec(shape, lambda i: (0, i, 0))


def _norm_mod_kernel(x_ref, g_ref, sh_ref, sc_ref, o_ref, *, pack):
    x = x_ref[...]
    y = x * lax.rsqrt(jnp.mean(x * x, axis=-1, keepdims=True) + EPS)
    h = (y * g_ref[...]) * (1.0 + sc_ref[...]) + sh_ref[...]
    if pack:
        o_ref[...] = _pack_bf16_pairs(h)
    else:
        o_ref[...] = h.astype(BF16)


def _norm_mod(x, g, shift, scale, seq, pack):
    m, d = x.shape
    tm = _pick(m if seq == 1 else seq, (512, 256, 128, 64, 32, 16, 8))
    out_d, out_t = (d // 2, U32) if pack else (d, BF16)
    return pl.pallas_call(
        functools.partial(_norm_mod_kernel, pack=pack),
        grid=(m // tm,),
        in_specs=[
            pl.BlockSpec((tm, d), lambda i: (i, 0)),
            pl.BlockSpec((1, d), lambda i: (0, 0)),
            _mod_spec(seq, tm, d, False),
            _mod_spec(seq, tm, d, False),
        ],
        out_specs=pl.BlockSpec((tm, out_d), lambda i: (i, 0)),
        out_shape=jax.ShapeDtypeStruct((m, out_d), out_t),
        compiler_params=_cparams(("arbitrary",)),
        name="norm_mod",
    )(x, g.reshape(1, d), shift, scale)


def _final_norm_kernel(x_ref, g_ref, o_ref):
    x = x_ref[...]
    y = x * lax.rsqrt(jnp.mean(x * x, axis=-1, keepdims=True) + EPS)
    o_ref[...] = y * g_ref[...]


def _final_norm(x, g):
    m, d = x.shape
    tm = _pick(m, (512, 256, 128, 64, 32, 16, 8))
    return pl.pallas_call(
        _final_norm_kernel,
        grid=(m // tm,),
        in_specs=[pl.BlockSpec((tm, d), lambda i: (i, 0)), pl.BlockSpec((1, d), lambda i: (0, 0))],
        out_specs=pl.BlockSpec((tm, d), lambda i: (i, 0)),
        out_shape=jax.ShapeDtypeStruct((m, d), F32),
        compiler_params=_cparams(("arbitrary",)),
        name="final_norm",
    )(x, g.reshape(1, d))


def _mm_plain_kernel(a_ref, w_ref, o_ref, wb_ref):
    @pl.when(pl.program_id(1) == 0)
    def _():
        wb_ref[...] = w_ref[...].astype(BF16)

    o_ref[...] = _dot(a_ref[...], wb_ref[...])


def _mm_plain(a, w, layer, tn_prefs):
    m, k = a.shape
    n = w.shape[2]
    tm = _pick(m, (512, 256, 128, 64, 32, 16))
    tn = _pick(n, tn_prefs)
    return pl.pallas_call(
        _mm_plain_kernel,
        grid=(n // tn, m // tm),
        in_specs=[
            pl.BlockSpec((tm, k), lambda j, i: (i, 0)),
            pl.BlockSpec((None, k, tn), lambda j, i: (layer, 0, j)),
        ],
        out_specs=pl.BlockSpec((tm, tn), lambda j, i: (i, j)),
        out_shape=jax.ShapeDtypeStruct((m, n), F32),
        scratch_shapes=[pltpu.VMEM((k, tn), BF16)],
        compiler_params=_cparams(("arbitrary", "arbitrary")),
        name="mm_plain",
    )(a, w)


def _mm_glu_kernel(a_ref, wa_ref, wg_ref, ba_ref, bg_ref, o_ref, wab_ref, wgb_ref):
    @pl.when(pl.program_id(1) == 0)
    def _():
        wab_ref[...] = wa_ref[...].astype(BF16)
        wgb_ref[...] = wg_ref[...].astype(BF16)

    a = a_ref[...]
    va = _dot(a, wab_ref[...]) + ba_ref[...]
    vg = _dot(a, wgb_ref[...]) + bg_ref[...]
    o_ref[...] = va * _sigmoid(vg)


def _mm_glu(a, w, b, layer):
    m, k = a.shape
    n = w.shape[2] // 2
    tm = _pick(m, (512, 256, 128, 64, 32, 16))
    tn = _pick(n, (512, 256, 128))
    nj = n // tn
    b3 = b.reshape(b.shape[0], 1, 2 * n)
    return pl.pallas_call(
        _mm_glu_kernel,
        grid=(nj, m // tm),
        in_specs=[
            pl.BlockSpec((tm, k), lambda j, i: (i, 0)),
            pl.BlockSpec((None, k, tn), lambda j, i: (layer, 0, j)),
            pl.BlockSpec((None, k, tn), lambda j, i: (layer, 0, j + nj)),
            pl.BlockSpec((None, 1, tn), lambda j, i: (layer, 0, j)),
            pl.BlockSpec((None, 1, tn), lambda j, i: (layer, 0, j + nj)),
        ],
        out_specs=pl.BlockSpec((tm, tn), lambda j, i: (i, j)),
        out_shape=jax.ShapeDtypeStruct((m, n), F32),
        scratch_shapes=[pltpu.VMEM((k, tn), BF16), pltpu.VMEM((k, tn), BF16)],
        compiler_params=_cparams(("arbitrary", "arbitrary")),
        name="mm_glu",
    )(a, w, w, b3, b3)


def _mm_res_kernel(a_ref, w_ref, b_ref, x_ref, gate_ref, o_ref, wb_ref):
    @pl.when(pl.program_id(1) == 0)
    def _():
        wb_ref[...] = w_ref[...].astype(BF16)

    out = _dot(a_ref[...], wb_ref[...]) + b_ref[...]
    o_ref[...] = x_ref[...] + gate_ref[...] * out


def _mm_res(a, w, b, x, gate, layer, seq):
    m, k = a.shape
    n = w.shape[2]
    tm = _pick(m if seq == 1 else seq, (512, 256, 128, 64, 32, 16))
    tn = _pick(n, (512, 256, 128))
    return pl.pallas_call(
        _mm_res_kernel,
        grid=(n // tn, m // tm),
        in_specs=[
            pl.BlockSpec((tm, k), lambda j, i: (i, 0)),
            pl.BlockSpec((None, k, tn), lambda j, i: (layer, 0, j)),
            pl.BlockSpec((1, tn), lambda j, i: (0, j)),
            pl.BlockSpec((tm, tn), lambda j, i: (i, j)),
            _mod_spec(seq, tm, tn, True),
        ],
        out_specs=pl.BlockSpec((tm, tn), lambda j, i: (i, j)),
        out_shape=jax.ShapeDtypeStruct((m, n), F32),
        scratch_shapes=[pltpu.VMEM((k, tn), BF16)],
        compiler_params=_cparams(("arbitrary", "arbitrary")),
        name="mm_res",
    )(a, w, b.reshape(1, n), x, gate)


def _gate_windows(d_rnn, bs):
    k0s = []
    for j in range(d_rnn // LANES):
        c0, c1 = LANES * j, LANES * (j + 1)
        n_lo, n_hi = c0 // bs, (c1 - 1) // bs
        k0 = min((n_lo * bs) // LANES * LANES, d_rnn - GATE_WINDOW)
        assert k0 <= n_lo * bs and (n_hi + 1) * bs <= k0 + GATE_WINDOW
        k0s.append(k0)
    return k0s


def _pack_gate_weights(w_a, w_x):
    nb, bs, _ = w_a.shape
    d_rnn = nb * bs
    eye = jnp.eye(nb, dtype=F32)

    def dense(w):
        return (w[:, :, None, :] * eye[:, None, :, None]).reshape(d_rnn, d_rnn)

    fa, fx = dense(w_a), dense(w_x)
    tiles = []
    for j, k0 in enumerate(_gate_windows(d_rnn, bs)):
        c0 = LANES * j
        tiles.append(jnp.concatenate([fa[k0:k0 + GATE_WINDOW, c0:c0 + LANES],
                                      fx[k0:k0 + GATE_WINDOW, c0:c0 + LANES]], axis=1))
    return jnp.stack(tiles).astype(BF16)


def _softplus(x):
    return jnp.maximum(x, 0.0) + jnp.log1p(jnp.exp(-jnp.abs(x)))


def _gelu_tanh(x):
    c = 0.7978845608028654
    return 0.5 * x * (1.0 + jnp.tanh(c * (x + 0.044715 * (x * x * x))))


def _causal_taps(x, w_ref, c0, c1, rows, first):
    n = x.shape[0]
    acc = None
    for r in range(SUBLANES):
        taps = [k for k in range(w_ref.shape[0]) if (first + k) % SUBLANES == r]
        if not taps:
            continue
        xr = x if r == 0 else pltpu.roll(x, n - r, 0)
        for k in taps:
            q = first + k - r
            term = w_ref[k:k + 1, c0:c1] * xr[q:q + rows, :]
            acc = term if acc is None else acc + term
    return acc


def _lru_gate_tile(xc, xb, wp_ref, ba_ref, bx_ref, sp, j, k0):
    c0, c1 = LANES * j, LANES * (j + 1)
    pre = _dot(xb[:, k0:k0 + GATE_WINDOW], wp_ref[j])
    r = _sigmoid(pre[:, :LANES] + ba_ref[:, c0:c1])
    i = _sigmoid(pre[:, LANES:] + bx_ref[:, c0:c1])
    log_a = (-LRU_C * r) * sp[:, c0:c1]
    a = jnp.exp(log_a)
    u = jnp.sqrt(1.0 - a * a) * (i * xc[:, c0:c1])
    return a, u


def _scan_rows(a, u, carry):
    rows = a.shape[0]
    groups = rows // SUBLANES
    a = a.reshape(groups, SUBLANES, LANES)
    u = u.reshape(groups, SUBLANES, LANES)
    sub = lax.broadcasted_iota(I32, a.shape, 1)
    d = 1
    while d < SUBLANES:
        keep = sub >= d
        a_s = jnp.where(keep, pltpu.roll(a, d, 1), 1.0)
        u_s = jnp.where(keep, pltpu.roll(u, d, 1), 0.0)
        u = a * u_s + u
        a = a * a_s
        d *= 2
    hs = []
    for g in range(groups):
        hg = a[g] * carry + u[g]
        hs.append(hg)
        carry = hg[SUBLANES - 1:SUBLANES, :]
    return jnp.concatenate(hs, axis=0), carry


def _lru_seq_kernel(gb_ref, main_ref, wc_ref, bc_ref, wp_ref, ba_ref, bx_ref, lam_ref,
                    yg_ref, hl_ref, xp_ref, xc_ref, xb_ref, hc_ref, *, tt, k0s):
    t = pl.program_id(1)
    d_rnn = main_ref.shape[1]

    @pl.when(t == 0)
    def _():
        xp_ref[0:SUBLANES, :] = jnp.zeros((SUBLANES, d_rnn), F32)
        hc_ref[...] = jnp.zeros_like(hc_ref)

    xp_ref[SUBLANES:, :] = main_ref[...]
    width = wc_ref.shape[0]
    for j in range(d_rnn // LANES):
        c0, c1 = LANES * j, LANES * (j + 1)
        acc = bc_ref[:, c0:c1] + _causal_taps(xp_ref[:, c0:c1], wc_ref, c0, c1, tt, SUBLANES - (width - 1))
        xc_ref[:, c0:c1] = acc
        xb_ref[:, c0:c1] = acc.astype(BF16)
    xp_ref[0:SUBLANES, :] = xp_ref[tt:tt + SUBLANES, :]

    sp = _softplus(-lam_ref[...])
    for j, k0 in enumerate(k0s):
        c0, c1 = LANES * j, LANES * (j + 1)
        a, u = _lru_gate_tile(xc_ref, xb_ref, wp_ref, ba_ref, bx_ref, sp, j, k0)
        h, carry = _scan_rows(a, u, hc_ref[:, c0:c1])
        hc_ref[:, c0:c1] = carry
        yg_ref[:, c0:c1] = (h * _gelu_tanh(gb_ref[:, c0:c1])).astype(BF16)

    @pl.when(t == pl.num_programs(1) - 1)
    def _():
        hl_ref[...] = hc_ref[...]


def _lru_seq(z, batch, seq, w_conv, b_conv, wp, b_a, b_x, lam, k0s):
    d_rnn = z.shape[1] // 2
    tt = _pick(seq, (256, 128, 64, 32, 16, 8))
    nt = seq // tt
    row = lambda b, t: b * nt + t
    full = lambda shape: pl.BlockSpec(shape, lambda b, t: (0,) * len(shape))
    yg, h_last = pl.pallas_call(
        functools.partial(_lru_seq_kernel, tt=tt, k0s=k0s),
        grid=(batch, nt),
        in_specs=[
            pl.BlockSpec((tt, d_rnn), lambda b, t: (row(b, t), 0)),
            pl.BlockSpec((tt, d_rnn), lambda b, t: (row(b, t), 1)),
            full((4, d_rnn)), full((1, d_rnn)), full(wp.shape),
            full((1, d_rnn)), full((1, d_rnn)), full((1, d_rnn)),
        ],
        out_specs=[
            pl.BlockSpec((tt, d_rnn), lambda b, t: (row(b, t), 0)),
            pl.BlockSpec((None, 1, d_rnn), lambda b, t: (b, 0, 0)),
        ],
        out_shape=[
            jax.ShapeDtypeStruct((batch * seq, d_rnn), BF16),
            jax.ShapeDtypeStruct((batch, 1, d_rnn), F32),
        ],
        scratch_shapes=[
            pltpu.VMEM((tt + SUBLANES, d_rnn), F32),
            pltpu.VMEM((tt, d_rnn), F32),
            pltpu.VMEM((tt, d_rnn), BF16),
            pltpu.VMEM((1, d_rnn), F32),
        ],
        compiler_params=_cparams(("arbitrary", "arbitrary")),
        name="lru_seq",
    )(z, z, w_conv, b_conv.reshape(1, -1), wp, b_a.reshape(1, -1), b_x.reshape(1, -1), lam.reshape(1, -1))
    return yg, h_last.reshape(batch, d_rnn)


def _lru_step_kernel(gb_ref, main_ref, p0_ref, p1_ref, p2_ref, h0_ref, wc_ref, bc_ref, wp_ref, ba_ref,
                     bx_ref, lam_ref, yg_ref, h_ref, xc_ref, xb_ref, *, k0s):
    acc = (bc_ref[...] + wc_ref[3:4, :] * main_ref[...] + wc_ref[2:3, :] * p2_ref[...]
           + wc_ref[1:2, :] * p1_ref[...] + wc_ref[0:1, :] * p0_ref[...])
    xc_ref[...] = acc
    xb_ref[...] = acc.astype(BF16)
    sp = _softplus(-lam_ref[...])
    for j, k0 in enumerate(k0s):
        c0, c1 = LANES * j, LANES * (j + 1)
        a, u = _lru_gate_tile(xc_ref, xb_ref, wp_ref, ba_ref, bx_ref, sp, j, k0)
        h = a * h0_ref[:, c0:c1] + u
        h_ref[:, c0:c1] = h
        yg_ref[:, c0:c1] = (h * _gelu_tanh(gb_ref[:, c0:c1])).astype(BF16)


def _lru_step(z, prev, h0, w_conv, b_conv, wp, b_a, b_x, lam, k0s):
    m = z.shape[0]
    d_rnn = z.shape[1] // 2
    tm = _pick(m, (128, 64, 32, 16))
    rows = lambda c: pl.BlockSpec((tm, d_rnn), lambda i: (i, c))
    full = lambda shape: pl.BlockSpec(shape, lambda i: (0,) * len(shape))
    return pl.pallas_call(
        functools.partial(_lru_step_kernel, k0s=k0s),
        grid=(m // tm,),
        in_specs=[
            rows(0), rows(1), rows(0), rows(0), rows(0), rows(0),
            full((4, d_rnn)), full((1, d_rnn)), full(wp.shape),
            full((1, d_rnn)), full((1, d_rnn)), full((1, d_rnn)),
        ],
        out_specs=[rows(0), rows(0)],
        out_shape=[jax.ShapeDtypeStruct((m, d_rnn), BF16), jax.ShapeDtypeStruct((m, d_rnn), F32)],
        scratch_shapes=[pltpu.VMEM((tm, d_rnn), F32), pltpu.VMEM((tm, d_rnn), BF16)],
        compiler_params=_cparams(("arbitrary",)),
        name="lru_step",
    )(z, z, prev[0], prev[1], prev[2], h0, w_conv, b_conv.reshape(1, -1), wp,
      b_a.reshape(1, -1), b_x.reshape(1, -1), lam.reshape(1, -1))


def _ln_silu(v, g, b):
    mu = jnp.mean(v, axis=-1, keepdims=True)
    dv = v - mu
    var = jnp.mean(dv * dv, axis=-1, keepdims=True)
    return _silu(dv * lax.rsqrt(var + EPS) * g + b)


def _conf_seq_kernel(u_ref, w_ref, b_ref, g_ref, bl_ref, o_ref, xp_ref, v_ref, *, tt, width, pad):
    d = u_ref.shape[1]

    @pl.when(pl.program_id(1) == 0)
    def _():
        xp_ref[0:pad, :] = jnp.zeros((pad, d), F32)

    xp_ref[pad:, :] = u_ref[...]
    for c in range(d // LANES):
        c0, c1 = LANES * c, LANES * (c + 1)
        v_ref[:, c0:c1] = b_ref[:, c0:c1] + _causal_taps(xp_ref[:, c0:c1], w_ref, c0, c1, tt, pad - (width - 1))
    xp_ref[0:pad, :] = xp_ref[tt:tt + pad, :]
    o_ref[...] = _ln_silu(v_ref[...], g_ref[...], bl_ref[...]).astype(BF16)


def _conf_seq(u, batch, seq, w_dw, b_dw, g_ln, b_ln):
    d = u.shape[1]
    width = w_dw.shape[0]
    pad = -(-(width - 1) // SUBLANES) * SUBLANES
    tt = _pick(seq, (128, 64, 32))
    nt = seq // tt
    full = lambda shape: pl.BlockSpec(shape, lambda b, t: (0,) * len(shape))
    return pl.pallas_call(
        functools.partial(_conf_seq_kernel, tt=tt, width=width, pad=pad),
        grid=(batch, nt),
        in_specs=[
            pl.BlockSpec((tt, d), lambda b, t: (b * nt + t, 0)),
            full((width, d)), full((1, d)), full((1, d)), full((1, d)),
        ],
        out_specs=pl.BlockSpec((tt, d), lambda b, t: (b * nt + t, 0)),
        out_shape=jax.ShapeDtypeStruct((batch * seq, d), BF16),
        scratch_shapes=[pltpu.VMEM((tt + pad, d), F32), pltpu.VMEM((tt, d), F32)],
        compiler_params=_cparams(("arbitrary", "arbitrary")),
        name="conf_seq",
    )(u, w_dw, b_dw.reshape(1, d), g_ln.reshape(1, d), b_ln.reshape(1, d))


def _conf_step_kernel(u_ref, buf_ref, w_ref, b_ref, g_ref, bl_ref, o_ref, *, width):
    acc = b_ref[...] + w_ref[width - 1:width, :] * u_ref[...]
    for k in range(width - 1):
        acc = acc + w_ref[k:k + 1, :] * buf_ref[k]
    o_ref[...] = _ln_silu(acc, g_ref[...], bl_ref[...]).astype(BF16)


def _conf_step(u, buf_t, w_dw, b_dw, g_ln, b_ln):
    m, d = u.shape
    width = w_dw.shape[0]
    tm = _pick(m, (32, 16, 8))
    full = lambda shape: pl.BlockSpec(shape, lambda i: (0,) * len(shape))
    return pl.pallas_call(
        functools.partial(_conf_step_kernel, width=width),
        grid=(m // tm,),
        in_specs=[
            pl.BlockSpec((tm, d), lambda i: (i, 0)),
            pl.BlockSpec((width - 1, tm, d), lambda i: (0, i, 0)),
            full((width, d)), full((1, d)), full((1, d)), full((1, d)),
        ],
        out_specs=pl.BlockSpec((tm, d), lambda i: (i, 0)),
        out_shape=jax.ShapeDtypeStruct((m, d), BF16),
        compiler_params=_cparams(("arbitrary",)),
        name="conf_step",
    )(u, buf_t, w_dw, b_dw.reshape(1, d), g_ln.reshape(1, d), b_ln.reshape(1, d))


def _lane_dense(cols, tm):
    lane = lax.broadcasted_iota(I32, (tm, LANES), 1)
    out = jnp.zeros((tm, LANES), cols[0].dtype)
    for k, c in enumerate(cols):
        out = jnp.where(lane == k, c, out)
    return out


def _router_kernel(x_ref, wr_ref, br_ref, idx_ref, wts_ref, rank_ref, cnt_ref, carry_ref, *, tm, n_exp):
    @pl.when(pl.program_id(0) == 0)
    def _():
        carry_ref[...] = jnp.zeros_like(carry_ref)

    x = _unpack_bf16_pairs(x_ref[...])
    probs = _sigmoid(_dot(x, wr_ref[...]))
    choice = probs + br_ref[...]
    lane = lax.broadcasted_iota(I32, (tm, n_exp), 1)
    neg = jnp.float32(-jnp.inf)
    big = jnp.int32(n_exp)
    gsz = n_exp // N_GROUPS

    in_grp, score = [], []
    for g in range(N_GROUPS):
        msk = (lane >= g * gsz) & (lane < (g + 1) * gsz)
        v = jnp.where(msk, choice, neg)
        m1 = jnp.max(v, axis=-1, keepdims=True)
        i1 = jnp.min(jnp.where(v == m1, lane, big), axis=-1, keepdims=True)
        m2 = jnp.max(jnp.where(lane == i1, neg, v), axis=-1, keepdims=True)
        in_grp.append(msk)
        score.append(m1 + m2)

    sel_lane = jnp.zeros((tm, n_exp), I32)
    for g in range(N_GROUPS):
        beaten = jnp.zeros((tm, 1), I32)
        for o in range(N_GROUPS):
            if o == g:
                continue
            wins = (score[o] >= score[g]) if o < g else (score[o] > score[g])
            beaten = beaten + wins.astype(I32)
        sel_lane = jnp.where(in_grp[g], (beaten < TOPK_GROUPS).astype(I32), sel_lane)

    masked = jnp.where(sel_lane > 0, choice, neg)
    idx, pk = [], []
    onehot = jnp.zeros((tm, n_exp), F32)
    for _ in range(TOP_K):
        m = jnp.max(masked, axis=-1, keepdims=True)
        ik = jnp.min(jnp.where(masked == m, lane, big), axis=-1, keepdims=True)
        hit = lane == ik
        pk.append(jnp.sum(jnp.where(hit, probs, 0.0), axis=-1, keepdims=True))
        idx.append(ik)
        onehot = jnp.where(hit, 1.0, onehot)
        masked = jnp.where(hit, neg, masked)

    total = pk[0]
    for p in pk[1:]:
        total = total + p
    wts = [p / total * ROUTED_SCALE for p in pk]

    r_i = lax.broadcasted_iota(I32, (tm, tm), 0)
    c_i = lax.broadcasted_iota(I32, (tm, tm), 1)
    tri = jnp.where(c_i < r_i, 1.0, 0.0).astype(BF16)
    before = _dot(tri, onehot.astype(BF16)) + carry_ref[...]
    rank = [jnp.sum(jnp.where(lane == ik, before, 0.0), axis=-1, keepdims=True).astype(I32) for ik in idx]
    carry_ref[...] = carry_ref[...] + jnp.sum(onehot, axis=0, keepdims=True)

    idx_ref[...] = _lane_dense(idx, tm)
    wts_ref[...] = _lane_dense(wts, tm)
    rank_ref[...] = _lane_dense(rank, tm)
    cnt_ref[...] = carry_ref[...]


def _router(h2p, w_router_b, b_router):
    n, dh = h2p.shape
    n_exp = w_router_b.shape[1]
    tm = _pick(n, (320, 256, 176, 128, 64, 48, 32, 16))
    outs = pl.pallas_call(
        functools.partial(_router_kernel, tm=tm, n_exp=n_exp),
        grid=(n // tm,),
        in_specs=[
            pl.BlockSpec((tm, dh), lambda i: (i, 0)),
            pl.BlockSpec((2 * dh, n_exp), lambda i: (0, 0)),
            pl.BlockSpec((1, n_exp), lambda i: (0, 0)),
        ],
        out_specs=[
            pl.BlockSpec((tm, LANES), lambda i: (i, 0)),
            pl.BlockSpec((tm, LANES), lambda i: (i, 0)),
            pl.BlockSpec((tm, LANES), lambda i: (i, 0)),
            pl.BlockSpec((1, n_exp), lambda i: (0, 0)),
        ],
        out_shape=[
            jax.ShapeDtypeStruct((n, LANES), I32),
            jax.ShapeDtypeStruct((n, LANES), F32),
            jax.ShapeDtypeStruct((n, LANES), I32),
            jax.ShapeDtypeStruct((1, n_exp), F32),
        ],
        scratch_shapes=[pltpu.VMEM((1, n_exp), F32)],
        compiler_params=_cparams(("arbitrary",)),
        name="moe_router",
    )(h2p, w_router_b, b_router.reshape(1, n_exp))
    return outs


def _dest_kernel(idx_ref, rank_ref, ps_ref, o_ref, *, tm, n_exp):
    lane = lax.broadcasted_iota(I32, (tm, n_exp), 1)
    idx = idx_ref[...]
    start = []
    for k in range(TOP_K):
        hit = lane == idx[:, k:k + 1]
        start.append(jnp.sum(jnp.where(hit, ps_ref[...], 0.0), axis=-1, keepdims=True).astype(I32))
    o_ref[...] = _lane_dense(start, tm) + rank_ref[...]


def _dest_rows(idx, rank, pstarts):
    n = idx.shape[0]
    n_exp = pstarts.shape[1]
    tm = _pick(n, (320, 256, 176, 128, 64, 48, 32, 16))
    return pl.pallas_call(
        functools.partial(_dest_kernel, tm=tm, n_exp=n_exp),
        grid=(n // tm,),
        in_specs=[
            pl.BlockSpec((tm, LANES), lambda i: (i, 0)),
            pl.BlockSpec((tm, LANES), lambda i: (i, 0)),
            pl.BlockSpec((1, n_exp), lambda i: (0, 0)),
        ],
        out_specs=pl.BlockSpec((tm, LANES), lambda i: (i, 0)),
        out_shape=jax.ShapeDtypeStruct((n, LANES), I32),
        compiler_params=_cparams(("arbitrary",)),
        name="moe_dest",
    )(idx, rank, pstarts)


def _dispatch_kernel(dest_ref, x_ref, wsg_ref, wsu_ref, wsd_ref, xs_hbm, sh_ref, wgb_ref, wub_ref, wdb_ref,
                     sem, *, tm):
    @pl.when(pl.program_id(0) == 0)
    def _():
        wgb_ref[...] = wsg_ref[...].astype(BF16)
        wub_ref[...] = wsu_ref[...].astype(BF16)
        wdb_ref[...] = wsd_ref[...].astype(BF16)

    def issue(t, carry):
        for k in range(TOP_K):
            d = dest_ref[0, t * TOP_K + k]
            pltpu.make_async_copy(x_ref.at[pl.ds(t, 1)], xs_hbm.at[pl.ds(d, 1)], sem).start(priority=k % 2)
        return carry

    lax.fori_loop(0, tm, issue, 0)

    x = _unpack_bf16_pairs(x_ref[...])
    act = _silu(_dot(x, wgb_ref[...])) * _dot(x, wub_ref[...])
    sh_ref[...] = _dot(act.astype(BF16), wdb_ref[...])

    for k in range(TOP_K):
        pltpu.make_async_copy(x_ref, xs_hbm.at[pl.ds(0, tm)], sem).wait()


def _dispatch_shared(h2p, dest, w_sg, w_su, w_sd, layer, n_rows):
    n, dh = h2p.shape
    d = 2 * dh
    de = w_sg.shape[2]
    tm = _pick(n, (320, 256, 176, 128, 64, 48, 32, 16))
    nt = n // tm
    xs, shared = pl.pallas_call(
        functools.partial(_dispatch_kernel, tm=tm),
        grid=(nt,),
        in_specs=[
            pl.BlockSpec((None, 1, tm * TOP_K), lambda i: (i, 0, 0), memory_space=pltpu.SMEM),
            pl.BlockSpec((tm, dh), lambda i: (i, 0)),
            pl.BlockSpec((None, d, de), lambda i: (layer, 0, 0)),
            pl.BlockSpec((None, d, de), lambda i: (layer, 0, 0)),
            pl.BlockSpec((None, de, d), lambda i: (layer, 0, 0)),
        ],
        out_specs=[
            pl.BlockSpec(memory_space=pl.ANY),
            pl.BlockSpec((tm, d), lambda i: (i, 0)),
        ],
        out_shape=[
            jax.ShapeDtypeStruct((n_rows, dh), U32),
            jax.ShapeDtypeStruct((n, d), F32),
        ],
        scratch_shapes=[
            pltpu.VMEM((d, de), BF16), pltpu.VMEM((d, de), BF16), pltpu.VMEM((de, d), BF16),
            pltpu.SemaphoreType.DMA(()),
        ],
        compiler_params=_cparams(("arbitrary",)),
        name="moe_dispatch",
    )(dest.reshape(nt, 1, tm * TOP_K), h2p, w_sg, w_su, w_sd)
    return xs, shared


def _expert_kernel(be_ref, bv_ref, na_ref, sz_ref, xs_ref, wg_hbm, wu_hbm, wd_hbm, y_ref,
                   wg_buf, wu_buf, wd_buf, wgb_ref, wub_ref, wdb_ref, sem, slot_ref, *, layer, n_exp):
    i = pl.program_id(0)
    e = be_ref[i]

    def weight_copies(expert, slot):
        return [
            pltpu.make_async_copy(wg_hbm.at[layer, expert], wg_buf.at[slot], sem.at[slot, 0]),
            pltpu.make_async_copy(wu_hbm.at[layer, expert], wu_buf.at[slot], sem.at[slot, 1]),
            pltpu.make_async_copy(wd_hbm.at[layer, expert], wd_buf.at[slot], sem.at[slot, 2]),
        ]

    def start_weights(expert, slot):
        for c, queue in zip(weight_copies(expert, slot), (0, 1, 1)):
            c.start(priority=queue)

    @pl.when(i == 0)
    def _():
        slot_ref[0] = 0
        start_weights(e, 0)

    is_first = (i == 0) | (e != be_ref[jnp.maximum(i - 1, 0)])

    @pl.when(is_first & (i < na_ref[0]))
    def _():
        slot = slot_ref[0]
        nxt = lax.while_loop(lambda n: (n < n_exp) & (sz_ref[jnp.minimum(n, n_exp - 1)] == 0),
                             lambda n: n + 1, e + 1)

        @pl.when(nxt < n_exp)
        def _():
            start_weights(nxt, 1 - slot)

        for c in weight_copies(e, slot):
            c.wait()
        wgb_ref[...] = wg_buf[slot].astype(BF16)
        wub_ref[...] = wu_buf[slot].astype(BF16)
        wdb_ref[...] = wd_buf[slot].astype(BF16)
        slot_ref[0] = 1 - slot

    @pl.when(i < na_ref[0])
    def _():
        x = _unpack_bf16_pairs(xs_ref[...])
        row = lax.broadcasted_iota(I32, x.shape, 0)
        x = jnp.where(row < bv_ref[i], x, jnp.zeros_like(x))
        act = _silu(_dot(x, wgb_ref[...])) * _dot(x, wub_ref[...])
        y_ref[...] = _dot(act.astype(BF16), wdb_ref[...])


def _experts(xs, blk_exp, blk_valid, n_active, sizes, w_eg, w_eu, w_ed, layer):
    n_rows, dh = xs.shape
    d = 2 * dh
    n_exp, _, de = w_eg.shape[1:]
    nb = n_rows // ROW_BLOCK
    blk = lambda i, be, bv, na, sz: (jnp.minimum(i, na[0] - 1), 0)
    grid_spec = pltpu.PrefetchScalarGridSpec(
        num_scalar_prefetch=4,
        grid=(nb,),
        in_specs=[
            pl.BlockSpec((ROW_BLOCK, dh), blk),
            pl.BlockSpec(memory_space=pl.ANY),
            pl.BlockSpec(memory_space=pl.ANY),
            pl.BlockSpec(memory_space=pl.ANY),
        ],
        out_specs=pl.BlockSpec((ROW_BLOCK, d), blk),
        scratch_shapes=[
            pltpu.VMEM((2, d, de), F32), pltpu.VMEM((2, d, de), F32), pltpu.VMEM((2, de, d), F32),
            pltpu.VMEM((d, de), BF16), pltpu.VMEM((d, de), BF16), pltpu.VMEM((de, d), BF16),
            pltpu.SemaphoreType.DMA((2, 3)),
            pltpu.SMEM((1,), I32),
        ],
    )
    return pl.pallas_call(
        functools.partial(_expert_kernel, layer=layer, n_exp=n_exp),
        grid_spec=grid_spec,
        out_shape=jax.ShapeDtypeStruct((n_rows, d), F32),
        compiler_params=_cparams(("arbitrary",)),
        name="moe_experts",
    )(blk_exp, blk_valid, n_active, sizes, xs, w_eg, w_eu, w_ed)


def _combine_kernel(dest_ref, dnext_ref, y_hbm, wts_ref, sh_ref, x_ref, gate_ref, o_ref, gbuf, sem, *, tm):
    i = pl.program_id(0)
    slot = lax.rem(i, 2)

    def issue(dref, s):
        def body(t, carry):
            for k in range(TOP_K):
                d = dref[0, t * TOP_K + k]
                pltpu.make_async_copy(y_hbm.at[pl.ds(d, 1)], gbuf.at[s, k, pl.ds(t, 1)],
                                      sem.at[s]).start(priority=k % 2)
            return carry

        lax.fori_loop(0, tm, body, 0)

    @pl.when(i == 0)
    def _():
        issue(dest_ref, 0)

    @pl.when(i + 1 < pl.num_programs(0))
    def _():
        issue(dnext_ref, 1 - slot)

    for k in range(TOP_K):
        pltpu.make_async_copy(y_hbm.at[pl.ds(0, tm)], gbuf.at[slot, k], sem.at[slot]).wait()

    w = wts_ref[...]
    f = sh_ref[...]
    for k in range(TOP_K):
        f = f + w[:, k:k + 1] * gbuf[slot, k]
    o_ref[...] = x_ref[...] + gate_ref[...] * f


def _combine(y_rows, dest, wts, shared, x, gate, seq, row_off):
    m, d = x.shape
    tm = _pick(m if seq == 1 else seq, (128, 64, 32, 16))
    off = row_off // tm
    nt = m // tm
    dest3 = dest[row_off:row_off + m].reshape(nt, 1, tm * TOP_K)
    return pl.pallas_call(
        functools.partial(_combine_kernel, tm=tm),
        grid=(nt,),
        in_specs=[
            pl.BlockSpec((None, 1, tm * TOP_K), lambda i: (i, 0, 0), memory_space=pltpu.SMEM),
            pl.BlockSpec((None, 1, tm * TOP_K), lambda i: (jnp.minimum(i + 1, nt - 1), 0, 0),
                         memory_space=pltpu.SMEM),
            pl.BlockSpec(memory_space=pl.ANY),
            pl.BlockSpec((tm, LANES), lambda i: (i + off, 0)),
            pl.BlockSpec((tm, d), lambda i: (i + off, 0)),
            pl.BlockSpec((tm, d), lambda i: (i, 0)),
            _mod_spec(seq, tm, d, False),
        ],
        out_specs=pl.BlockSpec((tm, d), lambda i: (i, 0)),
        out_shape=jax.ShapeDtypeStruct((m, d), F32),
        scratch_shapes=[pltpu.VMEM((2, TOP_K, tm, d), F32), pltpu.SemaphoreType.DMA((2,))],
        compiler_params=_cparams(("arbitrary",)),
        name="moe_combine",
    )(dest3, dest3, y_rows, wts, shared, x, gate)


def _moe(h2p, w_router_b, b_router, w_eg, w_eu, w_ed, w_sg, w_su, w_sd, layer):
    n = h2p.shape[0]
    n_exp = w_router_b.shape[1]
    idx, wts, rank, counts = _router(h2p, w_router_b, b_router)

    sizes = counts[0].astype(I32)
    padded = (sizes + ROW_BLOCK - 1) // ROW_BLOCK * ROW_BLOCK
    pends = jnp.cumsum(padded)
    pstarts = pends - padded
    n_blocks = -(-(n * TOP_K) // ROW_BLOCK) + n_exp
    n_active = (pends[-1] // ROW_BLOCK).astype(I32)
    experts = jnp.arange(n_exp, dtype=I32)
    last_exp = jnp.max(jnp.where(sizes > 0, experts, 0))
    blk_start = jnp.arange(n_blocks, dtype=I32) * ROW_BLOCK
    blk_exp = jnp.minimum(jnp.sum(pends[None, :] <= blk_start[:, None], axis=1).astype(I32), last_exp)
    seg_end = jnp.sum(jnp.where(blk_exp[:, None] == experts[None, :], (pstarts + sizes)[None, :], 0), axis=1)
    blk_valid = jnp.clip(seg_end - blk_start, 0, ROW_BLOCK).astype(I32)
    dest = _dest_rows(idx, rank, pstarts.astype(F32).reshape(1, n_exp))[:, :TOP_K]

    xs, shared = _dispatch_shared(h2p, dest, w_sg, w_su, w_sd, layer, n_blocks * ROW_BLOCK)
    y_rows = _experts(xs, blk_exp, blk_valid, n_active.reshape(1), sizes, w_eg, w_eu, w_ed, layer)
    return y_rows, dest, wts, shared


def _last_rows(x, batch, seq, n):
    return jnp.stack([x[(b + 1) * seq - n:(b + 1) * seq] for b in range(batch)])


def _group_mods(mod_l, b0, batch, seq, d):
    m = mod_l[b0:b0 + batch].reshape(batch, N_MOD, d)
    if seq > 1:
        return [m[:, k, :].reshape(batch, 1, d) for k in range(N_MOD)]
    return [m[:, k, :].reshape(1, batch, d) for k in range(N_MOD)]


def kernel(x_prompt, x_sample, state_lru_h, state_lru_conv, state_conv, c_prompt, c_sample, w_ada, b_ada, g_mix, g_ffn, w_lru_in, w_lru_conv, b_lru_conv, w_lru_a, b_lru_a, w_lru_x, b_lru_x, lru_lambda, w_lru_out, w_pw1, b_pw1, w_dw, b_dw, g_cln, b_cln, w_pw2, b_pw2, w_router, b_router, w_exp_gate, w_exp_up, w_exp_down, w_sh_gate, w_sh_up, w_sh_down, g_final):
    bp, tp, d = x_prompt.shape
    bs, ts, _ = x_sample.shape
    assert ts == 1
    depth = w_ada.shape[0]
    d_rnn = w_lru_in.shape[2] // 2
    np_, ns = bp * tp, bs * ts

    c_all = jnp.concatenate([c_prompt, c_sample], axis=0)
    pad_rows = -c_all.shape[0] % SUBLANES
    c_all = jnp.pad(c_all, ((0, pad_rows), (0, 0)))
    mod = _ada_mod(c_all, w_ada, b_ada)

    xp = x_prompt.reshape(np_, d)
    xs = x_sample.reshape(ns, d)
    k0s = _gate_windows(d_rnn, d_rnn // N_LRU_BLOCKS)
    zero_b = jnp.zeros((d,), F32)

    hs_p, hs_s, lcs_p, lcs_s, cbs_p, cbs_s = [], [], [], [], [], []
    for l in range(depth):
        mods_p = _group_mods(mod[l], 0, bp, tp, d)
        mods_s = _group_mods(mod[l], bp, bs, ts, d)
        j = l // 2
        hp = _norm_mod(xp, g_mix[l], mods_p[0], mods_p[1], tp, False)
        hsm = _norm_mod(xs, g_mix[l], mods_s[0], mods_s[1], ts, False)
        if l % 2 == 0:
            wp = _pack_gate_weights(w_lru_a[j], w_lru_x[j])
            lru_args = (w_lru_conv[j], b_lru_conv[j], wp, b_lru_a[j], b_lru_x[j], lru_lambda[j], k0s)
            zp = _mm_plain(hp, w_lru_in, j, (768, 896, 512, 384, 256, 128))
            zs = _mm_plain(hsm, w_lru_in, j, (768, 896, 512, 384, 256, 128))
            ygp, hlp = _lru_seq(zp, bp, tp, *lru_args)
            buf = state_lru_conv[j]
            ygs, hls = _lru_step(zs, (buf[:, 0], buf[:, 1], buf[:, 2]), state_lru_h[j], *lru_args)
            hs_p.append(hlp)
            hs_s.append(hls)
            lcs_p.append(_last_rows(zp[:, d_rnn:], bp, tp, 3))
            lcs_s.append(jnp.concatenate([buf[:, 1:], zs[:, None, d_rnn:]], axis=1))
            xp = _mm_res(ygp, w_lru_out, zero_b, xp, mods_p[2], j, tp)
            xs = _mm_res(ygs, w_lru_out, zero_b, xs, mods_s[2], j, ts)
        else:
            up = _mm_glu(hp, w_pw1, b_pw1, j)
            us = _mm_glu(hsm, w_pw1, b_pw1, j)
            conf_args = (w_dw[j], b_dw[j], g_cln[j], b_cln[j])
            vp = _conf_seq(up, bp, tp, *conf_args)
            buf = state_conv[j]
            vs = _conf_step(us, jnp.transpose(buf, (1, 0, 2)), *conf_args)
            width = w_dw.shape[1]
            cbs_p.append(_last_rows(up, bp, tp, width - 1))
            cbs_s.append(jnp.concatenate([buf[:, 1:], us[:, None, :]], axis=1))
            xp = _mm_res(vp, w_pw2, b_pw2[j], xp, mods_p[2], j, tp)
            xs = _mm_res(vs, w_pw2, b_pw2[j], xs, mods_s[2], j, ts)

        h2p = jnp.concatenate([
            _norm_mod(xp, g_ffn[l], mods_p[3], mods_p[4], tp, True),
            _norm_mod(xs, g_ffn[l], mods_s[3], mods_s[4], ts, True),
        ], axis=0)
        y_rows, dest, wts, shared = _moe(h2p, w_router[l].astype(BF16), b_router[l], w_exp_gate, w_exp_up,
                                         w_exp_down, w_sh_gate, w_sh_up, w_sh_down, l)
        xp = _combine(y_rows, dest, wts, shared, xp, mods_p[5], tp, 0)
        xs = _combine(y_rows, dest, wts, shared, xs, mods_s[5], ts, np_)

    y_prompt = _final_norm(xp, g_final).reshape(bp, tp, d)
    y_sample = _final_norm(xs, g_final).reshape(bs, ts, d)
    return (y_prompt, y_sample, jnp.stack(hs_p), jnp.stack(lcs_p), jnp.stack(cbs_p),
            jnp.stack(hs_s), jnp.stack(lcs_s), jnp.stack(cbs_s))
```

```python
import functools

import jax
import jax.numpy as jnp
from jax import lax
from jax.experimental import pallas as pl
from jax.experimental.pallas import tpu as pltpu

F32 = jnp.float32
BF16 = jnp.bfloat16
I32 = jnp.int32
U32 = jnp.uint32

N_LRU_BLOCKS = 16
LRU_C = 8.0
TOP_K = 8
N_GROUPS = 8
TOPK_GROUPS = 4
ROUTED_SCALE = 2.5
ROW_BLOCK = 128
N_MOD = 6
EPS = 1e-6

LANES = 128
SUBLANES = 8
GATE_WINDOW = 512
V7X_VMEM_BYTES = 64 * 1024 * 1024
VMEM_LIMIT = V7X_VMEM_BYTES - 8 * 1024 * 1024


def _pick(n, prefs):
    for p in prefs:
        if n % p == 0:
            return p
    return n


def _cparams(sem):
    return pltpu.CompilerParams(dimension_semantics=sem, vmem_limit_bytes=VMEM_LIMIT)


def _dot(a, b):
    return jnp.dot(a, b, preferred_element_type=F32)


def _sigmoid(x):
    return jax.nn.sigmoid(x)


def _silu(x):
    return x * jax.nn.sigmoid(x)


def _pack_bf16_pairs(h):
    half = h.shape[1] // 2
    bits = lax.bitcast_convert_type(h.astype(BF16).astype(F32), U32)
    return (bits[:, :half] >> 16) | (bits[:, half:] & jnp.uint32(0xFFFF0000))


def _unpack_bf16_pairs(w):
    lo = lax.bitcast_convert_type(w << 16, F32).astype(BF16)
    hi = lax.bitcast_convert_type(w & jnp.uint32(0xFFFF0000), F32).astype(BF16)
    return jnp.concatenate([lo, hi], axis=1)


def _ada_kernel(c_ref, w_ref, b_ref, o_ref):
    a = _silu(c_ref[...]).astype(BF16)
    o_ref[...] = _dot(a, w_ref[...].astype(BF16)) + b_ref[...]


def _ada_mod(c_all, w_ada, b_ada):
    depth, d, n = w_ada.shape
    bp = c_all.shape[0]
    tn = _pick(n, (1536, 1024, 512, 256, 128))
    return pl.pallas_call(
        _ada_kernel,
        grid=(depth, n // tn),
        in_specs=[
            pl.BlockSpec((bp, d), lambda l, j: (0, 0)),
            pl.BlockSpec((None, d, tn), lambda l, j: (l, 0, j)),
            pl.BlockSpec((None, 1, tn), lambda l, j: (l, 0, j)),
        ],
        out_specs=pl.BlockSpec((None, bp, tn), lambda l, j: (l, 0, j)),
        out_shape=jax.ShapeDtypeStruct((depth, bp, n), F32),
        compiler_params=_cparams(("arbitrary", "arbitrary")),
        name="ada_mod",
    )(c_all, w_ada, b_ada.reshape(depth, 1, n))


def _mod_spec(seq, tm, tn, col_axis_first):
    if seq > 1:
        shape = (None, 1, tn)
        if col_axis_first:
            return pl.BlockSpec(shape, lambda n, i: ((i * tm) // seq, 0, n))
        return pl.BlockSpec(shape, lambda i: ((i * tm) // seq, 0, 0))
    shape = (None, tm, tn)
    if col_axis_first:
        return pl.BlockSpec(shape, lambda n, i: (0, i, n))
    return pl.BlockSpec(shape, lambda i: (0, i, 0))


def _norm_mod_kernel(x_ref, g_ref, sh_ref, sc_ref, o_ref, *, pack):
    x = x_ref[...]
    y = x * lax.rsqrt(jnp.mean(x * x, axis=-1, keepdims=True) + EPS)
    h = (y * g_ref[...]) * (1.0 + sc_ref[...]) + sh_ref[...]
    if pack:
        o_ref[...] = _pack_bf16_pairs(h)
    else:
        o_ref[...] = h.astype(BF16)


def _norm_mod(x, g, shift, scale, seq, pack):
    m, d = x.shape
    tm = _pick(m if seq == 1 else seq, (512, 256, 128, 64, 32, 16, 8))
    out_d, out_t = (d // 2, U32) if pack else (d, BF16)
    return pl.pallas_call(
        functools.partial(_norm_mod_kernel, pack=pack),
        grid=(m // tm,),
        in_specs=[
            pl.BlockSpec((tm, d), lambda i: (i, 0)),
            pl.BlockSpec((1, d), lambda i: (0, 0)),
            _mod_spec(seq, tm, d, False),
            _mod_spec(seq, tm, d, False),
        ],
        out_specs=pl.BlockSpec((tm, out_d), lambda i: (i, 0)),
        out_shape=jax.ShapeDtypeStruct((m, out_d), out_t),
        compiler_params=_cparams(("arbitrary",)),
        name="norm_mod",
    )(x, g.reshape(1, d), shift, scale)


def _mm_plain_kernel(a_ref, w_ref, o_ref, wb_ref):
    @pl.when(pl.program_id(1) == 0)
    def _():
        wb_ref[...] = w_ref[...].astype(BF16)

    o_ref[...] = _dot(a_ref[...], wb_ref[...])


def _mm_plain(a, w, layer, tn_prefs):
    m, k = a.shape
    n = w.shape[2]
    tm = _pick(m, (512, 256, 128, 64, 32, 16))
    tn = _pick(n, tn_prefs)
    return pl.pallas_call(
        _mm_plain_kernel,
        grid=(n // tn, m // tm),
        in_specs=[
            pl.BlockSpec((tm, k), lambda j, i: (i, 0)),
            pl.BlockSpec((None, k, tn), lambda j, i: (layer, 0, j)),
        ],
        out_specs=pl.BlockSpec((tm, tn), lambda j, i: (i, j)),
        out_shape=jax.ShapeDtypeStruct((m, n), F32),
        scratch_shapes=[pltpu.VMEM((k, tn), BF16)],
        compiler_params=_cparams(("arbitrary", "arbitrary")),
        name="mm_plain",
    )(a, w)


def _mm_glu_kernel(a_ref, wa_ref, wg_ref, ba_ref, bg_ref, o_ref, wab_ref, wgb_ref):
    @pl.when(pl.program_id(1) == 0)
    def _():
        wab_ref[...] = wa_ref[...].astype(BF16)
        wgb_ref[...] = wg_ref[...].astype(BF16)

    a = a_ref[...]
    va = _dot(a, wab_ref[...]) + ba_ref[...]
    vg = _dot(a, wgb_ref[...]) + bg_ref[...]
    o_ref[...] = va * _sigmoid(vg)


def _mm_glu(a, w, b, layer):
    m, k = a.shape
    n = w.shape[2] // 2
    tm = _pick(m, (512, 256, 128, 64, 32, 16))
    tn = _pick(n, (512, 256, 128))
    nj = n // tn
    b3 = b.reshape(b.shape[0], 1, 2 * n)
    return pl.pallas_call(
        _mm_glu_kernel,
        grid=(nj, m // tm),
        in_specs=[
            pl.BlockSpec((tm, k), lambda j, i: (i, 0)),
            pl.BlockSpec((None, k, tn), lambda j, i: (layer, 0, j)),
            pl.BlockSpec((None, k, tn), lambda j, i: (layer, 0, j + nj)),
            pl.BlockSpec((None, 1, tn), lambda j, i: (layer, 0, j)),
            pl.BlockSpec((None, 1, tn), lambda j, i: (layer, 0, j + nj)),
        ],
        out_specs=pl.BlockSpec((tm, tn), lambda j, i: (i, j)),
        out_shape=jax.ShapeDtypeStruct((m, n), F32),
        scratch_shapes=[pltpu.VMEM((k, tn), BF16), pltpu.VMEM((k, tn), BF16)],
        compiler_params=_cparams(("arbitrary", "arbitrary")),
        name="mm_glu",
    )(a, w, w, b3, b3)


def _mm_res_kernel(a_ref, w_ref, b_ref, x_ref, gate_ref, o_ref, wb_ref):
    @pl.when(pl.program_id(1) == 0)
    def _():
        wb_ref[...] = w_ref[...].astype(BF16)

    out = _dot(a_ref[...], wb_ref[...]) + b_ref[...]
    o_ref[...] = x_ref[...] + gate_ref[...] * out


def _mm_res(a, w, b, x, gate, layer, seq):
    m, k = a.shape
    n = w.shape[2]
    tm = _pick(m if seq == 1 else seq, (512, 256, 128, 64, 32, 16))
    tn = _pick(n, (512, 256, 128))
    return pl.pallas_call(
        _mm_res_kernel,
        grid=(n // tn, m // tm),
        in_specs=[
            pl.BlockSpec((tm, k), lambda j, i: (i, 0)),
            pl.BlockSpec((None, k, tn), lambda j, i: (layer, 0, j)),
            pl.BlockSpec((1, tn), lambda j, i: (0, j)),
            pl.BlockSpec((tm, tn), lambda j, i: (i, j)),
            _mod_spec(seq, tm, tn, True),
        ],
        out_specs=pl.BlockSpec((tm, tn), lambda j, i: (i, j)),
        out_shape=jax.ShapeDtypeStruct((m, n), F32),
        scratch_shapes=[pltpu.VMEM((k, tn), BF16)],
        compiler_params=_cparams(("arbitrary", "arbitrary")),
        name="mm_res",
    )(a, w, b.reshape(1, n), x, gate)


def _gate_windows(d_rnn, bs):
    k0s = []
    for j in range(d_rnn // LANES):
        c0, c1 = LANES * j, LANES * (j + 1)
        n_lo, n_hi = c0 // bs, (c1 - 1) // bs
        k0 = min((n_lo * bs) // LANES * LANES, d_rnn - GATE_WINDOW)
        assert k0 <= n_lo * bs and (n_hi + 1) * bs <= k0 + GATE_WINDOW
        k0s.append(k0)
    return k0s


def _pack_gate_weights(w_a, w_x):
    nb, bs, _ = w_a.shape
    d_rnn = nb * bs
    blk = jnp.arange(d_rnn, dtype=I32) // bs
    on_diag = blk[:, None] == blk[None, :]

    def dense(w):
        return jnp.where(on_diag, jnp.tile(w.reshape(d_rnn, bs), (1, nb)), 0.0)

    fa, fx = dense(w_a), dense(w_x)
    tiles = []
    for j, k0 in enumerate(_gate_windows(d_rnn, bs)):
        c0 = LANES * j
        tiles.append(jnp.concatenate([fa[k0:k0 + GATE_WINDOW, c0:c0 + LANES],
                                      fx[k0:k0 + GATE_WINDOW, c0:c0 + LANES]], axis=1))
    return jnp.stack(tiles).astype(BF16)


def _softplus(x):
    return jnp.maximum(x, 0.0) + jnp.log1p(jnp.exp(-jnp.abs(x)))


def _gelu_tanh(x):
    c = 0.7978845608028654
    return 0.5 * x * (1.0 + jnp.tanh(c * (x + 0.044715 * (x * x * x))))


def _causal_taps(x, w_ref, c0, c1, rows, first):
    n = x.shape[0]
    acc = None
    for r in range(SUBLANES):
        taps = [k for k in range(w_ref.shape[0]) if (first + k) % SUBLANES == r]
        if not taps:
            continue
        xr = x if r == 0 else pltpu.roll(x, n - r, 0)
        for k in taps:
            q = first + k - r
            term = w_ref[k:k + 1, c0:c1] * xr[q:q + rows, :]
            acc = term if acc is None else acc + term
    return acc


def _lru_gate_tile(xc, xb, wp_ref, ba_ref, bx_ref, sp, j, k0):
    c0, c1 = LANES * j, LANES * (j + 1)
    pre = _dot(xb[:, k0:k0 + GATE_WINDOW], wp_ref[j])
    r = _sigmoid(pre[:, :LANES] + ba_ref[:, c0:c1])
    i = _sigmoid(pre[:, LANES:] + bx_ref[:, c0:c1])
    log_a = (-LRU_C * r) * sp[:, c0:c1]
    a = jnp.exp(log_a)
    u = jnp.sqrt(1.0 - a * a) * (i * xc[:, c0:c1])
    return a, u


def _scan_rows(a, u, carry):
    rows = a.shape[0]
    groups = rows // SUBLANES
    a = a.reshape(groups, SUBLANES, LANES)
    u = u.reshape(groups, SUBLANES, LANES)
    sub = lax.broadcasted_iota(I32, a.shape, 1)
    d = 1
    while d < SUBLANES:
        keep = sub >= d
        a_s = jnp.where(keep, pltpu.roll(a, d, 1), 1.0)
        u_s = jnp.where(keep, pltpu.roll(u, d, 1), 0.0)
        u = a * u_s + u
        a = a * a_s
        d *= 2
    hs = []
    for g in range(groups):
        hg = a[g] * carry + u[g]
        hs.append(hg)
        carry = hg[SUBLANES - 1:SUBLANES, :]
    return jnp.concatenate(hs, axis=0), carry


def _lru_seq_kernel(gb_ref, main_ref, wc_ref, bc_ref, wp_ref, ba_ref, bx_ref, lam_ref,
                    yg_ref, hl_ref, xp_ref, xc_ref, xb_ref, hc_ref, *, tt, k0s):
    t = pl.program_id(1)
    d_rnn = main_ref.shape[1]

    @pl.when(t == 0)
    def _():
        xp_ref[0:SUBLANES, :] = jnp.zeros((SUBLANES, d_rnn), F32)
        hc_ref[...] = jnp.zeros_like(hc_ref)

    xp_ref[SUBLANES:, :] = main_ref[...]
    width = wc_ref.shape[0]
    for j in range(d_rnn // LANES):
        c0, c1 = LANES * j, LANES * (j + 1)
        acc = bc_ref[:, c0:c1] + _causal_taps(xp_ref[:, c0:c1], wc_ref, c0, c1, tt, SUBLANES - (width - 1))
        xc_ref[:, c0:c1] = acc
        xb_ref[:, c0:c1] = acc.astype(BF16)
    xp_ref[0:SUBLANES, :] = xp_ref[tt:tt + SUBLANES, :]

    sp = _softplus(-lam_ref[...])
    for j, k0 in enumerate(k0s):
        c0, c1 = LANES * j, LANES * (j + 1)
        a, u = _lru_gate_tile(xc_ref, xb_ref, wp_ref, ba_ref, bx_ref, sp, j, k0)
        h, carry = _scan_rows(a, u, hc_ref[:, c0:c1])
        hc_ref[:, c0:c1] = carry
        yg_ref[:, c0:c1] = (h * _gelu_tanh(gb_ref[:, c0:c1])).astype(BF16)

    @pl.when(t == pl.num_programs(1) - 1)
    def _():
        hl_ref[...] = hc_ref[...]


def _lru_seq(z, batch, seq, w_conv, b_conv, wp, b_a, b_x, lam, k0s):
    d_rnn = z.shape[1] // 2
    tt = _pick(seq, (256, 128, 64, 32, 16, 8))
    nt = seq // tt
    row = lambda b, t: b * nt + t
    full = lambda shape: pl.BlockSpec(shape, lambda b, t: (0,) * len(shape))
    yg, h_last = pl.pallas_call(
        functools.partial(_lru_seq_kernel, tt=tt, k0s=k0s),
        grid=(batch, nt),
        in_specs=[
            pl.BlockSpec((tt, d_rnn), lambda b, t: (row(b, t), 0)),
            pl.BlockSpec((tt, d_rnn), lambda b, t: (row(b, t), 1)),
            full((4, d_rnn)), full((1, d_rnn)), full(wp.shape),
            full((1, d_rnn)), full((1, d_rnn)), full((1, d_rnn)),
        ],
        out_specs=[
            pl.BlockSpec((tt, d_rnn), lambda b, t: (row(b, t), 0)),
            pl.BlockSpec((None, 1, d_rnn), lambda b, t: (b, 0, 0)),
        ],
        out_shape=[
            jax.ShapeDtypeStruct((batch * seq, d_rnn), BF16),
            jax.ShapeDtypeStruct((batch, 1, d_rnn), F32),
        ],
        scratch_shapes=[
            pltpu.VMEM((tt + SUBLANES, d_rnn), F32),
            pltpu.VMEM((tt, d_rnn), F32),
            pltpu.VMEM((tt, d_rnn), BF16),
            pltpu.VMEM((1, d_rnn), F32),
        ],
        compiler_params=_cparams(("arbitrary", "arbitrary")),
        name="lru_seq",
    )(z, z, w_conv, b_conv.reshape(1, -1), wp, b_a.reshape(1, -1), b_x.reshape(1, -1), lam.reshape(1, -1))
    return yg, h_last.reshape(batch, d_rnn)


def _lru_step_kernel(gb_ref, main_ref, p0_ref, p1_ref, p2_ref, h0_ref, wc_ref, bc_ref, wp_ref, ba_ref,
                     bx_ref, lam_ref, yg_ref, h_ref, xc_ref, xb_ref, *, k0s):
    acc = (bc_ref[...] + wc_ref[3:4, :] * main_ref[...] + wc_ref[2:3, :] * p2_ref[...]
           + wc_ref[1:2, :] * p1_ref[...] + wc_ref[0:1, :] * p0_ref[...])
    xc_ref[...] = acc
    xb_ref[...] = acc.astype(BF16)
    sp = _softplus(-lam_ref[...])
    for j, k0 in enumerate(k0s):
        c0, c1 = LANES * j, LANES * (j + 1)
        a, u = _lru_gate_tile(xc_ref, xb_ref, wp_ref, ba_ref, bx_ref, sp, j, k0)
        h = a * h0_ref[:, c0:c1] + u
        h_ref[:, c0:c1] = h
        yg_ref[:, c0:c1] = (h * _gelu_tanh(gb_ref[:, c0:c1])).astype(BF16)


def _lru_step(z, prev, h0, w_conv, b_conv, wp, b_a, b_x, lam, k0s):
    m = z.shape[0]
    d_rnn = z.shape[1] // 2
    tm = _pick(m, (128, 64, 32, 16))
    rows = lambda c: pl.BlockSpec((tm, d_rnn), lambda i: (i, c))
    full = lambda shape: pl.BlockSpec(shape, lambda i: (0,) * len(shape))
    return pl.pallas_call(
        functools.partial(_lru_step_kernel, k0s=k0s),
        grid=(m // tm,),
        in_specs=[
            rows(0), rows(1), rows(0), rows(0), rows(0), rows(0),
            full((4, d_rnn)), full((1, d_rnn)), full(wp.shape),
            full((1, d_rnn)), full((1, d_rnn)), full((1, d_rnn)),
        ],
        out_specs=[rows(0), rows(0)],
        out_shape=[jax.ShapeDtypeStruct((m, d_rnn), BF16), jax.ShapeDtypeStruct((m, d_rnn), F32)],
        scratch_shapes=[pltpu.VMEM((tm, d_rnn), F32), pltpu.VMEM((tm, d_rnn), BF16)],
        compiler_params=_cparams(("arbitrary",)),
        name="lru_step",
    )(z, z, prev[0], prev[1], prev[2], h0, w_conv, b_conv.reshape(1, -1), wp,
      b_a.reshape(1, -1), b_x.reshape(1, -1), lam.reshape(1, -1))


def _ln_silu(v, g, b):
    mu = jnp.mean(v, axis=-1, keepdims=True)
    dv = v - mu
    var = jnp.mean(dv * dv, axis=-1, keepdims=True)
    return _silu(dv * lax.rsqrt(var + EPS) * g + b)


def _conf_seq_kernel(u_ref, w_ref, b_ref, g_ref, bl_ref, o_ref, xp_ref, v_ref, *, tt, width, pad):
    d = u_ref.shape[1]

    @pl.when(pl.program_id(1) == 0)
    def _():
        xp_ref[0:pad, :] = jnp.zeros((pad, d), F32)

    xp_ref[pad:, :] = u_ref[...]
    for c in range(d // LANES):
        c0, c1 = LANES * c, LANES * (c + 1)
        v_ref[:, c0:c1] = b_ref[:, c0:c1] + _causal_taps(xp_ref[:, c0:c1], w_ref, c0, c1, tt, pad - (width - 1))
    xp_ref[0:pad, :] = xp_ref[tt:tt + pad, :]
    o_ref[...] = _ln_silu(v_ref[...], g_ref[...], bl_ref[...]).astype(BF16)


def _conf_seq(u, batch, seq, w_dw, b_dw, g_ln, b_ln):
    d = u.shape[1]
    width = w_dw.shape[0]
    pad = -(-(width - 1) // SUBLANES) * SUBLANES
    tt = _pick(seq, (128, 64, 32))
    nt = seq // tt
    full = lambda shape: pl.BlockSpec(shape, lambda b, t: (0,) * len(shape))
    return pl.pallas_call(
        functools.partial(_conf_seq_kernel, tt=tt, width=width, pad=pad),
        grid=(batch, nt),
        in_specs=[
            pl.BlockSpec((tt, d), lambda b, t: (b * nt + t, 0)),
            full((width, d)), full((1, d)), full((1, d)), full((1, d)),
        ],
        out_specs=pl.BlockSpec((tt, d), lambda b, t: (b * nt + t, 0)),
        out_shape=jax.ShapeDtypeStruct((batch * seq, d), BF16),
        scratch_shapes=[pltpu.VMEM((tt + pad, d), F32), pltpu.VMEM((tt, d), F32)],
        compiler_params=_cparams(("arbitrary", "arbitrary")),
        name="conf_seq",
    )(u, w_dw, b_dw.reshape(1, d), g_ln.reshape(1, d), b_ln.reshape(1, d))


def _conf_step_kernel(u_ref, buf_ref, w_ref, b_ref, g_ref, bl_ref, o_ref, *, width):
    acc = b_ref[...] + w_ref[width - 1:width, :] * u_ref[...]
    for k in range(width - 1):
        acc = acc + w_ref[k:k + 1, :] * buf_ref[k]
    o_ref[...] = _ln_silu(acc, g_ref[...], bl_ref[...]).astype(BF16)


def _conf_step(u, buf_t, w_dw, b_dw, g_ln, b_ln):
    m, d = u.shape
    width = w_dw.shape[0]
    tm = _pick(m, (32, 16, 8))
    full = lambda shape: pl.BlockSpec(shape, lambda i: (0,) * len(shape))
    return pl.pallas_call(
        functools.partial(_conf_step_kernel, width=width),
        grid=(m // tm,),
        in_specs=[
            pl.BlockSpec((tm, d), lambda i: (i, 0)),
            pl.BlockSpec((width - 1, tm, d), lambda i: (0, i, 0)),
            full((width, d)), full((1, d)), full((1, d)), full((1, d)),
        ],
        out_specs=pl.BlockSpec((tm, d), lambda i: (i, 0)),
        out_shape=jax.ShapeDtypeStruct((m, d), BF16),
        compiler_params=_cparams(("arbitrary",)),
        name="conf_step",
    )(u, buf_t, w_dw, b_dw.reshape(1, d), g_ln.reshape(1, d), b_ln.reshape(1, d))


def _rows_dense(rows, tn):
    sub = lax.broadcasted_iota(I32, (SUBLANES, tn), 0)
    out = jnp.zeros((SUBLANES, tn), rows[0].dtype)
    for k, r in enumerate(rows):
        out = jnp.where(sub == k, r, out)
    return out


def _expert_iota(rows, tn, first):
    return lax.broadcasted_iota(I32, (rows, tn), 0).astype(F32) + float(first)


def _router_kernel(x_ref, wrt_ref, br_ref, idx_ref, wts_ref, rank_ref, cnt_ref, carry_ref, *, tn, n_exp):
    @pl.when(pl.program_id(0) == 0)
    def _():
        carry_ref[...] = jnp.zeros_like(carry_ref)

    x = _unpack_bf16_pairs(x_ref[...])
    logits = lax.dot_general(wrt_ref[...], x, (((1,), (1,)), ((), ())), preferred_element_type=F32)
    probs = _sigmoid(logits)
    choice = probs + br_ref[...]
    row = _expert_iota(n_exp, tn, 0)
    neg = jnp.float32(-jnp.inf)
    big = jnp.float32(n_exp)
    gsz = n_exp // N_GROUPS

    score = []
    for g in range(N_GROUPS):
        v = choice[g * gsz:(g + 1) * gsz, :]
        rg = _expert_iota(gsz, tn, g * gsz)
        m1 = jnp.max(v, axis=0, keepdims=True)
        i1 = jnp.min(jnp.where(v == m1, rg, big), axis=0, keepdims=True)
        m2 = jnp.max(jnp.where(rg == i1, neg, v), axis=0, keepdims=True)
        score.append(m1 + m2)

    masked = []
    for g in range(N_GROUPS):
        beaten = jnp.zeros((1, tn), I32)
        for o in range(N_GROUPS):
            if o == g:
                continue
            wins = (score[o] >= score[g]) if o < g else (score[o] > score[g])
            beaten = beaten + wins.astype(I32)
        masked.append(jnp.where(beaten < TOPK_GROUPS, choice[g * gsz:(g + 1) * gsz, :], neg))
    masked = jnp.concatenate(masked, axis=0)

    idx, pk = [], []
    onehot = jnp.zeros((n_exp, tn), F32)
    for _ in range(TOP_K):
        m = jnp.max(masked, axis=0, keepdims=True)
        ik = jnp.min(jnp.where(masked == m, row, big), axis=0, keepdims=True)
        hit = row == ik
        pk.append(jnp.sum(jnp.where(hit, probs, 0.0), axis=0, keepdims=True))
        idx.append(ik)
        onehot = jnp.where(hit, 1.0, onehot)
        masked = jnp.where(hit, neg, masked)

    total = pk[0]
    for p in pk[1:]:
        total = total + p
    wts = [p / total * ROUTED_SCALE for p in pk]

    r_i = lax.broadcasted_iota(I32, (tn, tn), 0)
    c_i = lax.broadcasted_iota(I32, (tn, tn), 1)
    tri = jnp.where(r_i < c_i, 1.0, 0.0).astype(BF16)
    before = _dot(onehot.astype(BF16), tri) + carry_ref[...]
    rank = [jnp.sum(jnp.where(row == ik, before, 0.0), axis=0, keepdims=True).astype(I32) for ik in idx]
    carry_ref[...] = carry_ref[...] + jnp.sum(onehot, axis=1, keepdims=True)

    idx_ref[...] = _rows_dense(idx, tn).astype(I32)
    wts_ref[...] = _rows_dense(wts, tn)
    rank_ref[...] = _rows_dense(rank, tn)
    cnt_ref[...] = carry_ref[...]


def _token_tile(n):
    return _pick(n, (640, 512, 384, 256, 128))


def _router(h2p, w_router_t, b_router):
    n, dh = h2p.shape
    n_exp = w_router_t.shape[0]
    tn = _token_tile(n)
    row8 = pl.BlockSpec((TOP_K, tn), lambda i: (0, i))
    return pl.pallas_call(
        functools.partial(_router_kernel, tn=tn, n_exp=n_exp),
        grid=(n // tn,),
        in_specs=[
            pl.BlockSpec((tn, dh), lambda i: (i, 0)),
            pl.BlockSpec((n_exp, 2 * dh), lambda i: (0, 0)),
            pl.BlockSpec((n_exp, 1), lambda i: (0, 0)),
        ],
        out_specs=[row8, row8, row8, pl.BlockSpec((n_exp, 1), lambda i: (0, 0))],
        out_shape=[
            jax.ShapeDtypeStruct((TOP_K, n), I32),
            jax.ShapeDtypeStruct((TOP_K, n), F32),
            jax.ShapeDtypeStruct((TOP_K, n), I32),
            jax.ShapeDtypeStruct((n_exp, 1), F32),
        ],
        scratch_shapes=[pltpu.VMEM((n_exp, 1), F32)],
        compiler_params=_cparams(("arbitrary",)),
        name="moe_router",
    )(h2p, w_router_t, b_router.reshape(n_exp, 1))


def _dest_kernel(idx_ref, rank_ref, ps_ref, o_ref, *, tn, n_exp):
    row = lax.broadcasted_iota(I32, (n_exp, tn), 0)
    idx = idx_ref[...]
    start = []
    for k in range(TOP_K):
        hit = row == idx[k:k + 1, :]
        start.append(jnp.sum(jnp.where(hit, ps_ref[...], 0.0), axis=0, keepdims=True).astype(I32))
    o_ref[...] = _rows_dense(start, tn) + rank_ref[...]


def _dest_rows(idx, rank, pstarts):
    n = idx.shape[1]
    n_exp = pstarts.shape[0]
    tn = _token_tile(n)
    row8 = pl.BlockSpec((TOP_K, tn), lambda i: (0, i))
    return pl.pallas_call(
        functools.partial(_dest_kernel, tn=tn, n_exp=n_exp),
        grid=(n // tn,),
        in_specs=[row8, row8, pl.BlockSpec((n_exp, 1), lambda i: (0, 0))],
        out_specs=row8,
        out_shape=jax.ShapeDtypeStruct((TOP_K, n), I32),
        compiler_params=_cparams(("arbitrary",)),
        name="moe_dest",
    )(idx, rank, pstarts)


def _dispatch_kernel(dest_ref, x_ref, wsg_ref, wsu_ref, wsd_ref, xs_hbm, sh_ref, wgb_ref, wub_ref, wdb_ref,
                     sem, *, tm):
    @pl.when(pl.program_id(0) == 0)
    def _():
        wgb_ref[...] = wsg_ref[...].astype(BF16)
        wub_ref[...] = wsu_ref[...].astype(BF16)
        wdb_ref[...] = wsd_ref[...].astype(BF16)

    def issue(t, carry):
        for k in range(TOP_K):
            d = dest_ref[0, t * TOP_K + k]
            pltpu.make_async_copy(x_ref.at[pl.ds(t, 1)], xs_hbm.at[pl.ds(d, 1)], sem).start(priority=k % 2)
        return carry

    lax.fori_loop(0, tm, issue, 0)

    x = _unpack_bf16_pairs(x_ref[...])
    act = _silu(_dot(x, wgb_ref[...])) * _dot(x, wub_ref[...])
    sh_ref[...] = _dot(act.astype(BF16), wdb_ref[...])

    for k in range(TOP_K):
        pltpu.make_async_copy(x_ref, xs_hbm.at[pl.ds(0, tm)], sem).wait()


def _dispatch_shared(h2p, dest, w_sg, w_su, w_sd, layer, n_rows):
    n, dh = h2p.shape
    d = 2 * dh
    de = w_sg.shape[2]
    tm = _pick(n, (320, 256, 176, 128, 64, 48, 32, 16))
    nt = n // tm
    xs, shared = pl.pallas_call(
        functools.partial(_dispatch_kernel, tm=tm),
        grid=(nt,),
        in_specs=[
            pl.BlockSpec((None, 1, tm * TOP_K), lambda i: (i, 0, 0), memory_space=pltpu.SMEM),
            pl.BlockSpec((tm, dh), lambda i: (i, 0)),
            pl.BlockSpec((None, d, de), lambda i: (layer, 0, 0)),
            pl.BlockSpec((None, d, de), lambda i: (layer, 0, 0)),
            pl.BlockSpec((None, de, d), lambda i: (layer, 0, 0)),
        ],
        out_specs=[
            pl.BlockSpec(memory_space=pl.ANY),
            pl.BlockSpec((tm, d), lambda i: (i, 0)),
        ],
        out_shape=[
            jax.ShapeDtypeStruct((n_rows, dh), U32),
            jax.ShapeDtypeStruct((n, d), F32),
        ],
        scratch_shapes=[
            pltpu.VMEM((d, de), BF16), pltpu.VMEM((d, de), BF16), pltpu.VMEM((de, d), BF16),
            pltpu.SemaphoreType.DMA(()),
        ],
        compiler_params=_cparams(("arbitrary",)),
        name="moe_dispatch",
    )(dest.reshape(nt, 1, tm * TOP_K), h2p, w_sg, w_su, w_sd)
    return xs, shared


def _expert_kernel(be_ref, bv_ref, na_ref, sz_ref, xs_ref, wg_hbm, wu_hbm, wd_hbm, y_ref,
                   wg_buf, wu_buf, wd_buf, wgb_ref, wub_ref, wdb_ref, sem, slot_ref, *, layer, n_exp):
    i = pl.program_id(0)
    e = be_ref[i]

    def weight_copies(expert, slot):
        return [
            pltpu.make_async_copy(wg_hbm.at[layer, expert], wg_buf.at[slot], sem.at[slot, 0]),
            pltpu.make_async_copy(wu_hbm.at[layer, expert], wu_buf.at[slot], sem.at[slot, 1]),
            pltpu.make_async_copy(wd_hbm.at[layer, expert], wd_buf.at[slot], sem.at[slot, 2]),
        ]

    def start_weights(expert, slot):
        for c, queue in zip(weight_copies(expert, slot), (0, 1, 1)):
            c.start(priority=queue)

    @pl.when(i == 0)
    def _():
        slot_ref[0] = 0
        start_weights(e, 0)

    is_first = (i == 0) | (e != be_ref[jnp.maximum(i - 1, 0)])

    @pl.when(is_first & (i < na_ref[0]))
    def _():
        slot = slot_ref[0]
        nxt = lax.while_loop(lambda n: (n < n_exp) & (sz_ref[jnp.minimum(n, n_exp - 1)] == 0),
                             lambda n: n + 1, e + 1)

        @pl.when(nxt < n_exp)
        def _():
            start_weights(nxt, 1 - slot)

        for c in weight_copies(e, slot):
            c.wait()
        wgb_ref[...] = wg_buf[slot].astype(BF16)
        wub_ref[...] = wu_buf[slot].astype(BF16)
        wdb_ref[...] = wd_buf[slot].astype(BF16)
        slot_ref[0] = 1 - slot

    @pl.when(i < na_ref[0])
    def _():
        x = _unpack_bf16_pairs(xs_ref[...])
        row = lax.broadcasted_iota(I32, x.shape, 0)
        x = jnp.where(row < bv_ref[i], x, jnp.zeros_like(x))
        act = _silu(_dot(x, wgb_ref[...])) * _dot(x, wub_ref[...])
        y_ref[...] = _dot(act.astype(BF16), wdb_ref[...])


def _experts(xs, blk_exp, blk_valid, n_active, sizes, w_eg, w_eu, w_ed, layer):
    n_rows, dh = xs.shape
    d = 2 * dh
    n_exp, _, de = w_eg.shape[1:]
    nb = n_rows // ROW_BLOCK
    blk = lambda i, be, bv, na, sz: (jnp.minimum(i, na[0] - 1), 0)
    grid_spec = pltpu.PrefetchScalarGridSpec(
        num_scalar_prefetch=4,
        grid=(nb,),
        in_specs=[
            pl.BlockSpec((ROW_BLOCK, dh), blk),
            pl.BlockSpec(memory_space=pl.ANY),
            pl.BlockSpec(memory_space=pl.ANY),
            pl.BlockSpec(memory_space=pl.ANY),
        ],
        out_specs=pl.BlockSpec((ROW_BLOCK, d), blk),
        scratch_shapes=[
            pltpu.VMEM((2, d, de), F32), pltpu.VMEM((2, d, de), F32), pltpu.VMEM((2, de, d), F32),
            pltpu.VMEM((d, de), BF16), pltpu.VMEM((d, de), BF16), pltpu.VMEM((de, d), BF16),
            pltpu.SemaphoreType.DMA((2, 3)),
            pltpu.SMEM((1,), I32),
        ],
    )
    return pl.pallas_call(
        functools.partial(_expert_kernel, layer=layer, n_exp=n_exp),
        grid_spec=grid_spec,
        out_shape=jax.ShapeDtypeStruct((n_rows, d), F32),
        compiler_params=_cparams(("arbitrary",)),
        name="moe_experts",
    )(blk_exp, blk_valid, n_active, sizes, xs, w_eg, w_eu, w_ed)


COMBINE_ROWS = 16


def _combine_kernel(dest_ref, dnext_ref, y_hbm, wts_ref, sh_ref, x_ref, gate_ref, *rest, tm, post):
    if post == "mix":
        g_ref, shift_ref, scale_ref, o_ref, h_ref, gbuf, sem = rest
    elif post == "final":
        g_ref, h_ref, gbuf, sem = rest
    else:
        o_ref, gbuf, sem = rest
    i = pl.program_id(0)
    slot = lax.rem(i, 2)

    def start_row(dref, s, t):
        for k in range(TOP_K):
            pltpu.make_async_copy(y_hbm.at[pl.ds(dref[0, t * TOP_K + k], 1)], gbuf.at[s, k, pl.ds(t, 1)],
                                  sem.at[s]).start(priority=k % 2)

    def wait_slot(s):
        for k in range(TOP_K):
            pltpu.make_async_copy(y_hbm.at[pl.ds(0, tm)], gbuf.at[s, k], sem.at[s]).wait()

    @pl.when(i == 0)
    def _():
        def body(t, carry):
            start_row(dest_ref, 0, t)
            return carry

        lax.fori_loop(0, tm, body, 0)

    wait_slot(slot)

    def rows_of(ref, rows):
        return ref[...] if ref.shape[0] == 1 else ref[rows, :]

    for q in range(tm // COMBINE_ROWS):
        for t in range(q * COMBINE_ROWS, (q + 1) * COMBINE_ROWS):
            start_row(dnext_ref, 1 - slot, t)
        rows = slice(q * COMBINE_ROWS, (q + 1) * COMBINE_ROWS)
        w = wts_ref[rows, :]
        f = sh_ref[rows, :]
        for k in range(TOP_K):
            f = f + w[:, k:k + 1] * gbuf[slot, k, rows, :]
        xn = x_ref[rows, :] + rows_of(gate_ref, rows) * f
        if post != "final":
            o_ref[rows, :] = xn
        if post is not None:
            y = xn * lax.rsqrt(jnp.mean(xn * xn, axis=-1, keepdims=True) + EPS) * g_ref[...]
            if post == "mix":
                y = y * (1.0 + rows_of(scale_ref, rows)) + rows_of(shift_ref, rows)
            h_ref[rows, :] = y.astype(h_ref.dtype)

    @pl.when(i == pl.num_programs(0) - 1)
    def _():
        wait_slot(1 - slot)


def _combine(y_rows, dest, wts, shared, x, gate, seq, row_off, post=None, g=None, shift=None, scale=None):
    m, d = x.shape
    tm = _pick(m if seq == 1 else seq, (128, 64, 32, 16))
    off = row_off // tm
    nt = m // tm
    dest3 = dest[row_off:row_off + m].reshape(nt, 1, tm * TOP_K)
    smem = lambda nxt: pl.BlockSpec((None, 1, tm * TOP_K), lambda i: (jnp.minimum(i + nxt, nt - 1), 0, 0),
                                    memory_space=pltpu.SMEM)
    tile = pl.BlockSpec((tm, d), lambda i: (i, 0))
    in_specs = [
        smem(0), smem(1),
        pl.BlockSpec(memory_space=pl.ANY),
        pl.BlockSpec((tm, TOP_K), lambda i: (i + off, 0)),
        pl.BlockSpec((tm, d), lambda i: (i + off, 0)),
        tile,
        _mod_spec(seq, tm, d, False),
    ]
    args = [dest3, dest3, y_rows, wts, shared, x, gate]
    x_out = jax.ShapeDtypeStruct((m, d), F32)
    if post == "mix":
        in_specs += [pl.BlockSpec((1, d), lambda i: (0, 0)), _mod_spec(seq, tm, d, False), _mod_spec(seq, tm, d, False)]
        args += [g.reshape(1, d), shift, scale]
        out_specs, out_shape = [tile, tile], [x_out, jax.ShapeDtypeStruct((m, d), BF16)]
    elif post == "final":
        in_specs += [pl.BlockSpec((1, d), lambda i: (0, 0))]
        args += [g.reshape(1, d)]
        out_specs, out_shape = tile, x_out
    else:
        out_specs, out_shape = tile, x_out
    return pl.pallas_call(
        functools.partial(_combine_kernel, tm=tm, post=post),
        grid=(nt,),
        in_specs=in_specs,
        out_specs=out_specs,
        out_shape=out_shape,
        scratch_shapes=[pltpu.VMEM((2, TOP_K, tm, d), F32), pltpu.SemaphoreType.DMA((2,))],
        compiler_params=_cparams(("arbitrary",)),
        name="moe_combine",
    )(*args)


def _moe(h2p, w_router_b, b_router, w_eg, w_eu, w_ed, w_sg, w_su, w_sd, layer):
    n = h2p.shape[0]
    n_exp = w_router_b.shape[0]
    idx, wts, rank, counts = _router(h2p, w_router_b, b_router)

    sizes = counts[:, 0].astype(I32)
    padded = (sizes + ROW_BLOCK - 1) // ROW_BLOCK * ROW_BLOCK
    pends = jnp.cumsum(padded)
    pstarts = pends - padded
    n_blocks = -(-(n * TOP_K) // ROW_BLOCK) + n_exp
    n_active = (pends[-1] // ROW_BLOCK).astype(I32)
    experts = jnp.arange(n_exp, dtype=I32)
    last_exp = jnp.max(jnp.where(sizes > 0, experts, 0))
    blk_start = jnp.arange(n_blocks, dtype=I32) * ROW_BLOCK
    blk_exp = jnp.minimum(jnp.sum(pends[None, :] <= blk_start[:, None], axis=1).astype(I32), last_exp)
    seg_end = jnp.sum(jnp.where(blk_exp[:, None] == experts[None, :], (pstarts + sizes)[None, :], 0), axis=1)
    blk_valid = jnp.clip(seg_end - blk_start, 0, ROW_BLOCK).astype(I32)
    dest = _dest_rows(idx, rank, pstarts.astype(F32).reshape(n_exp, 1)).T

    xs, shared = _dispatch_shared(h2p, dest, w_sg, w_su, w_sd, layer, n_blocks * ROW_BLOCK)
    y_rows = _experts(xs, blk_exp, blk_valid, n_active.reshape(1), sizes, w_eg, w_eu, w_ed, layer)
    return y_rows, dest, wts.T, shared


def _last_rows(x, batch, seq, n):
    return jnp.stack([x[(b + 1) * seq - n:(b + 1) * seq] for b in range(batch)])


def _group_mods(mod_l, b0, batch, seq, d):
    m = mod_l[b0:b0 + batch].reshape(batch, N_MOD, d)
    if seq > 1:
        return [m[:, k, :].reshape(batch, 1, d) for k in range(N_MOD)]
    return [m[:, k, :].reshape(1, batch, d) for k in range(N_MOD)]


def kernel(x_prompt, x_sample, state_lru_h, state_lru_conv, state_conv, c_prompt, c_sample, w_ada, b_ada, g_mix, g_ffn, w_lru_in, w_lru_conv, b_lru_conv, w_lru_a, b_lru_a, w_lru_x, b_lru_x, lru_lambda, w_lru_out, w_pw1, b_pw1, w_dw, b_dw, g_cln, b_cln, w_pw2, b_pw2, w_router, b_router, w_exp_gate, w_exp_up, w_exp_down, w_sh_gate, w_sh_up, w_sh_down, g_final):
    bp, tp, d = x_prompt.shape
    bs, ts, _ = x_sample.shape
    assert ts == 1
    depth = w_ada.shape[0]
    d_rnn = w_lru_in.shape[2] // 2
    np_, ns = bp * tp, bs * ts

    c_all = jnp.concatenate([c_prompt, c_sample], axis=0)
    pad_rows = -c_all.shape[0] % SUBLANES
    c_all = jnp.pad(c_all, ((0, pad_rows), (0, 0)))
    mod = _ada_mod(c_all, w_ada, b_ada)

    xp = x_prompt.reshape(np_, d)
    xs = x_sample.reshape(ns, d)
    k0s = _gate_windows(d_rnn, d_rnn // N_LRU_BLOCKS)
    zero_b = jnp.zeros((d,), F32)

    hs_p, hs_s, lcs_p, lcs_s, cbs_p, cbs_s = [], [], [], [], [], []
    all_mods = [(_group_mods(mod[l], 0, bp, tp, d), _group_mods(mod[l], bp, bs, ts, d)) for l in range(depth)]
    hp = _norm_mod(xp, g_mix[0], all_mods[0][0][0], all_mods[0][0][1], tp, False)
    hsm = _norm_mod(xs, g_mix[0], all_mods[0][1][0], all_mods[0][1][1], ts, False)
    for l in range(depth):
        mods_p, mods_s = all_mods[l]
        j = l // 2
        if l % 2 == 0:
            wp = _pack_gate_weights(w_lru_a[j], w_lru_x[j])
            lru_args = (w_lru_conv[j], b_lru_conv[j], wp, b_lru_a[j], b_lru_x[j], lru_lambda[j], k0s)
            zp = _mm_plain(hp, w_lru_in, j, (768, 896, 512, 384, 256, 128))
            zs = _mm_plain(hsm, w_lru_in, j, (768, 896, 512, 384, 256, 128))
            ygp, hlp = _lru_seq(zp, bp, tp, *lru_args)
            buf = state_lru_conv[j]
            ygs, hls = _lru_step(zs, (buf[:, 0], buf[:, 1], buf[:, 2]), state_lru_h[j], *lru_args)
            hs_p.append(hlp)
            hs_s.append(hls)
            lcs_p.append(_last_rows(zp[:, d_rnn:], bp, tp, 3))
            lcs_s.append(jnp.concatenate([buf[:, 1:], zs[:, None, d_rnn:]], axis=1))
            xp = _mm_res(ygp, w_lru_out, zero_b, xp, mods_p[2], j, tp)
            xs = _mm_res(ygs, w_lru_out, zero_b, xs, mods_s[2], j, ts)
        else:
            up = _mm_glu(hp, w_pw1, b_pw1, j)
            us = _mm_glu(hsm, w_pw1, b_pw1, j)
            conf_args = (w_dw[j], b_dw[j], g_cln[j], b_cln[j])
            vp = _conf_seq(up, bp, tp, *conf_args)
            buf = state_conv[j]
            vs = _conf_step(us, jnp.transpose(buf, (1, 0, 2)), *conf_args)
            width = w_dw.shape[1]
            cbs_p.append(_last_rows(up, bp, tp, width - 1))
            cbs_s.append(jnp.concatenate([buf[:, 1:], us[:, None, :]], axis=1))
            xp = _mm_res(vp, w_pw2, b_pw2[j], xp, mods_p[2], j, tp)
            xs = _mm_res(vs, w_pw2, b_pw2[j], xs, mods_s[2], j, ts)

        h2p = jnp.concatenate([
            _norm_mod(xp, g_ffn[l], mods_p[3], mods_p[4], tp, True),
            _norm_mod(xs, g_ffn[l], mods_s[3], mods_s[4], ts, True),
        ], axis=0)
        y_rows, dest, wts, shared = _moe(h2p, w_router[l].T.astype(BF16), b_router[l], w_exp_gate, w_exp_up,
                                         w_exp_down, w_sh_gate, w_sh_up, w_sh_down, l)
        moe_p = (y_rows, dest, wts, shared, xp, mods_p[5], tp, 0)
        moe_s = (y_rows, dest, wts, shared, xs, mods_s[5], ts, np_)
        if l + 1 < depth:
            nxt_p, nxt_s = all_mods[l + 1]
            xp, hp = _combine(*moe_p, post="mix", g=g_mix[l + 1], shift=nxt_p[0], scale=nxt_p[1])
            xs, hsm = _combine(*moe_s, post="mix", g=g_mix[l + 1], shift=nxt_s[0], scale=nxt_s[1])
        else:
            y_prompt = _combine(*moe_p, post="final", g=g_final).reshape(bp, tp, d)
            y_sample = _combine(*moe_s, post="final", g=g_final).reshape(bs, ts, d)

    return (y_prompt, y_sample, jnp.stack(hs_p), jnp.stack(lcs_p), jnp.stack(cbs_p),
            jnp.stack(hs_s), jnp.stack(lcs_s), jnp.stack(cbs_s))
```

```python
import functools

import jax
import jax.numpy as jnp
from jax import lax
from jax.experimental import pallas as pl
from jax.experimental.pallas import tpu as pltpu

F32 = jnp.float32
BF16 = jnp.bfloat16
I32 = jnp.int32
U32 = jnp.uint32

N_LRU_BLOCKS = 16
LRU_C = 8.0
TOP_K = 8
N_GROUPS = 8
TOPK_GROUPS = 4
ROUTED_SCALE = 2.5
ROW_BLOCK = 128
N_MOD = 6
EPS = 1e-6

LANES = 128
SUBLANES = 8
GATE_WINDOW = 512
V7X_VMEM_BYTES = 64 * 1024 * 1024
VMEM_LIMIT = V7X_VMEM_BYTES - 8 * 1024 * 1024


def _pick(n, prefs):
    for p in prefs:
        if n % p == 0:
            return p
    return n


def _cparams(sem):
    return pltpu.CompilerParams(dimension_semantics=sem, vmem_limit_bytes=VMEM_LIMIT)


def _dot(a, b):
    return jnp.dot(a, b, preferred_element_type=F32)


def _sigmoid(x):
    return jax.nn.sigmoid(x)


def _silu(x):
    return x * jax.nn.sigmoid(x)


def _pack_bf16_pairs(h):
    half = h.shape[1] // 2
    bits = lax.bitcast_convert_type(h.astype(BF16).astype(F32), U32)
    return (bits[:, :half] >> 16) | (bits[:, half:] & jnp.uint32(0xFFFF0000))


def _unpack_bf16_pairs(w):
    lo = lax.bitcast_convert_type(w << 16, F32).astype(BF16)
    hi = lax.bitcast_convert_type(w & jnp.uint32(0xFFFF0000), F32).astype(BF16)
    return jnp.concatenate([lo, hi], axis=1)


def _ada_kernel(c_ref, w_ref, b_ref, o_ref):
    a = _silu(c_ref[...]).astype(BF16)
    o_ref[...] = _dot(a, w_ref[...].astype(BF16)) + b_ref[...]


def _ada_mod(c_all, w_ada, b_ada):
    depth, d, n = w_ada.shape
    bp = c_all.shape[0]
    tn = _pick(n, (1536, 1024, 512, 256, 128))
    return pl.pallas_call(
        _ada_kernel,
        grid=(depth, n // tn),
        in_specs=[
            pl.BlockSpec((bp, d), lambda l, j: (0, 0)),
            pl.BlockSpec((None, d, tn), lambda l, j: (l, 0, j)),
            pl.BlockSpec((None, 1, tn), lambda l, j: (l, 0, j)),
        ],
        out_specs=pl.BlockSpec((None, bp, tn), lambda l, j: (l, 0, j)),
        out_shape=jax.ShapeDtypeStruct((depth, bp, n), F32),
        compiler_params=_cparams(("arbitrary", "arbitrary")),
        name="ada_mod",
    )(c_all, w_ada, b_ada.reshape(depth, 1, n))


def _mod_spec(seq, tm, tn, col_axis_first):
    if seq > 1:
        shape = (None, 1, tn)
        if col_axis_first:
            return pl.BlockSpec(shape, lambda n, i: ((i * tm) // seq, 0, n))
        return pl.BlockSpec(shape, lambda i: ((i * tm) // seq, 0, 0))
    shape = (None, tm, tn)
    if col_axis_first:
        return pl.BlockSpec(shape, lambda n, i: (0, i, n))
    return pl.BlockSpec(shape, lambda i: (0, i, 0))


def _norm_mod_kernel(x_ref, g_ref, sh_ref, sc_ref, o_ref, *, pack):
    x = x_ref[...]
    y = x * lax.rsqrt(jnp.mean(x * x, axis=-1, keepdims=True) + EPS)
    h = (y * g_ref[...]) * (1.0 + sc_ref[...]) + sh_ref[...]
    if pack:
        o_ref[...] = _pack_bf16_pairs(h)
    else:
        o_ref[...] = h.astype(BF16)


def _norm_mod(x, g, shift, scale, seq, pack):
    m, d = x.shape
    tm = _pick(m if seq == 1 else seq, (512, 256, 128, 64, 32, 16, 8))
    out_d, out_t = (d // 2, U32) if pack else (d, BF16)
    return pl.pallas_call(
        functools.partial(_norm_mod_kernel, pack=pack),
        grid=(m // tm,),
        in_specs=[
            pl.BlockSpec((tm, d), lambda i: (i, 0)),
            pl.BlockSpec((1, d), lambda i: (0, 0)),
            _mod_spec(seq, tm, d, False),
            _mod_spec(seq, tm, d, False),
        ],
        out_specs=pl.BlockSpec((tm, out_d), lambda i: (i, 0)),
        out_shape=jax.ShapeDtypeStruct((m, out_d), out_t),
        compiler_params=_cparams(("arbitrary",)),
        name="norm_mod",
    )(x, g.reshape(1, d), shift, scale)


def _mm_plain_kernel(a_ref, w_ref, o_ref, wb_ref):
    @pl.when(pl.program_id(1) == 0)
    def _():
        wb_ref[...] = w_ref[...].astype(BF16)

    o_ref[...] = _dot(a_ref[...], wb_ref[...])


def _mm_plain(a, w, layer, tn_prefs):
    m, k = a.shape
    n = w.shape[2]
    tm = _pick(m, (1024, 512, 256, 128, 64, 32, 16))
    tn = _pick(n, tn_prefs)
    return pl.pallas_call(
        _mm_plain_kernel,
        grid=(n // tn, m // tm),
        in_specs=[
            pl.BlockSpec((tm, k), lambda j, i: (i, 0)),
            pl.BlockSpec((None, k, tn), lambda j, i: (layer, 0, j)),
        ],
        out_specs=pl.BlockSpec((tm, tn), lambda j, i: (i, j)),
        out_shape=jax.ShapeDtypeStruct((m, n), F32),
        scratch_shapes=[pltpu.VMEM((k, tn), BF16)],
        compiler_params=_cparams(("arbitrary", "arbitrary")),
        name="mm_plain",
    )(a, w)


def _mm_glu_kernel(a_ref, wa_ref, wg_ref, ba_ref, bg_ref, o_ref, wab_ref, wgb_ref):
    @pl.when(pl.program_id(1) == 0)
    def _():
        wab_ref[...] = wa_ref[...].astype(BF16)
        wgb_ref[...] = wg_ref[...].astype(BF16)

    a = a_ref[...]
    va = _dot(a, wab_ref[...]) + ba_ref[...]
    vg = _dot(a, wgb_ref[...]) + bg_ref[...]
    o_ref[...] = va * _sigmoid(vg)


def _mm_glu(a, w, b, layer):
    m, k = a.shape
    n = w.shape[2] // 2
    tm = _pick(m, (1024, 512, 256, 128, 64, 32, 16))
    tn = _pick(n, (512, 256, 128))
    nj = n // tn
    b3 = b.reshape(b.shape[0], 1, 2 * n)
    return pl.pallas_call(
        _mm_glu_kernel,
        grid=(nj, m // tm),
        in_specs=[
            pl.BlockSpec((tm, k), lambda j, i: (i, 0)),
            pl.BlockSpec((None, k, tn), lambda j, i: (layer, 0, j)),
            pl.BlockSpec((None, k, tn), lambda j, i: (layer, 0, j + nj)),
            pl.BlockSpec((None, 1, tn), lambda j, i: (layer, 0, j)),
            pl.BlockSpec((None, 1, tn), lambda j, i: (layer, 0, j + nj)),
        ],
        out_specs=pl.BlockSpec((tm, tn), lambda j, i: (i, j)),
        out_shape=jax.ShapeDtypeStruct((m, n), F32),
        scratch_shapes=[pltpu.VMEM((k, tn), BF16), pltpu.VMEM((k, tn), BF16)],
        compiler_params=_cparams(("arbitrary", "arbitrary")),
        name="mm_glu",
    )(a, w, w, b3, b3)


def _mm_res_kernel(a_ref, w_ref, b_ref, x_ref, gate_ref, o_ref, wb_ref):
    @pl.when(pl.program_id(1) == 0)
    def _():
        wb_ref[...] = w_ref[...].astype(BF16)

    out = _dot(a_ref[...], wb_ref[...]) + b_ref[...]
    o_ref[...] = x_ref[...] + gate_ref[...] * out


def _mm_res(a, w, b, x, gate, layer, seq):
    m, k = a.shape
    n = w.shape[2]
    tm = _pick(m if seq == 1 else seq, (512, 256, 128, 64, 32, 16))
    tn = _pick(n, (1024, 512, 256, 128))
    return pl.pallas_call(
        _mm_res_kernel,
        grid=(n // tn, m // tm),
        in_specs=[
            pl.BlockSpec((tm, k), lambda j, i: (i, 0)),
            pl.BlockSpec((None, k, tn), lambda j, i: (layer, 0, j)),
            pl.BlockSpec((1, tn), lambda j, i: (0, j)),
            pl.BlockSpec((tm, tn), lambda j, i: (i, j)),
            _mod_spec(seq, tm, tn, True),
        ],
        out_specs=pl.BlockSpec((tm, tn), lambda j, i: (i, j)),
        out_shape=jax.ShapeDtypeStruct((m, n), F32),
        scratch_shapes=[pltpu.VMEM((k, tn), BF16)],
        compiler_params=_cparams(("arbitrary", "arbitrary")),
        name="mm_res",
    )(a, w, b.reshape(1, n), x, gate)


def _gate_windows(d_rnn, bs):
    k0s = []
    for j in range(d_rnn // LANES):
        c0, c1 = LANES * j, LANES * (j + 1)
        n_lo, n_hi = c0 // bs, (c1 - 1) // bs
        k0 = min((n_lo * bs) // LANES * LANES, d_rnn - GATE_WINDOW)
        assert k0 <= n_lo * bs and (n_hi + 1) * bs <= k0 + GATE_WINDOW
        k0s.append(k0)
    return k0s


def _pack_gate_weights(w_a, w_x):
    nb, bs, _ = w_a.shape
    d_rnn = nb * bs
    blk = jnp.arange(d_rnn, dtype=I32) // bs
    on_diag = blk[:, None] == blk[None, :]

    def dense(w):
        return jnp.where(on_diag, jnp.tile(w.reshape(d_rnn, bs), (1, nb)), 0.0)

    fa, fx = dense(w_a), dense(w_x)
    tiles = []
    for j, k0 in enumerate(_gate_windows(d_rnn, bs)):
        c0 = LANES * j
        tiles.append(jnp.concatenate([fa[k0:k0 + GATE_WINDOW, c0:c0 + LANES],
                                      fx[k0:k0 + GATE_WINDOW, c0:c0 + LANES]], axis=1))
    return jnp.stack(tiles).astype(BF16)


def _softplus(x):
    return jnp.maximum(x, 0.0) + jnp.log1p(jnp.exp(-jnp.abs(x)))


def _gelu_tanh(x):
    c = 0.7978845608028654
    return 0.5 * x * (1.0 + jnp.tanh(c * (x + 0.044715 * (x * x * x))))


def _causal_taps(x, w_ref, c0, c1, rows, first):
    n = x.shape[0]
    acc = None
    for r in range(SUBLANES):
        taps = [k for k in range(w_ref.shape[0]) if (first + k) % SUBLANES == r]
        if not taps:
            continue
        xr = x if r == 0 else pltpu.roll(x, n - r, 0)
        for k in taps:
            q = first + k - r
            term = w_ref[k:k + 1, c0:c1] * xr[q:q + rows, :]
            acc = term if acc is None else acc + term
    return acc


def _lru_gate_tile(xc, xb, wp_ref, ba_ref, bx_ref, sp, j, k0):
    c0, c1 = LANES * j, LANES * (j + 1)
    pre = _dot(xb[:, k0:k0 + GATE_WINDOW], wp_ref[j])
    r = _sigmoid(pre[:, :LANES] + ba_ref[:, c0:c1])
    i = _sigmoid(pre[:, LANES:] + bx_ref[:, c0:c1])
    log_a = (-LRU_C * r) * sp[:, c0:c1]
    a = jnp.exp(log_a)
    u = jnp.sqrt(1.0 - a * a) * (i * xc[:, c0:c1])
    return a, u


def _scan_rows(a, u, carry):
    rows = a.shape[0]
    groups = rows // SUBLANES
    a = a.reshape(groups, SUBLANES, LANES)
    u = u.reshape(groups, SUBLANES, LANES)
    sub = lax.broadcasted_iota(I32, a.shape, 1)
    d = 1
    while d < SUBLANES:
        keep = sub >= d
        a_s = jnp.where(keep, pltpu.roll(a, d, 1), 1.0)
        u_s = jnp.where(keep, pltpu.roll(u, d, 1), 0.0)
        u = a * u_s + u
        a = a * a_s
        d *= 2
    hs = []
    for g in range(groups):
        hg = a[g] * carry + u[g]
        hs.append(hg)
        carry = hg[SUBLANES - 1:SUBLANES, :]
    return jnp.concatenate(hs, axis=0), carry


def _lru_seq_kernel(gb_ref, main_ref, wc_ref, bc_ref, wp_ref, ba_ref, bx_ref, lam_ref,
                    yg_ref, hl_ref, xp_ref, xc_ref, xb_ref, hc_ref, *, tt, k0s):
    t = pl.program_id(1)
    d_rnn = main_ref.shape[1]

    @pl.when(t == 0)
    def _():
        xp_ref[0:SUBLANES, :] = jnp.zeros((SUBLANES, d_rnn), F32)
        hc_ref[...] = jnp.zeros_like(hc_ref)

    xp_ref[SUBLANES:, :] = main_ref[...]
    width = wc_ref.shape[0]
    for j in range(d_rnn // LANES):
        c0, c1 = LANES * j, LANES * (j + 1)
        acc = bc_ref[:, c0:c1] + _causal_taps(xp_ref[:, c0:c1], wc_ref, c0, c1, tt, SUBLANES - (width - 1))
        xc_ref[:, c0:c1] = acc
        xb_ref[:, c0:c1] = acc.astype(BF16)
    xp_ref[0:SUBLANES, :] = xp_ref[tt:tt + SUBLANES, :]

    sp = _softplus(-lam_ref[...])
    for j, k0 in enumerate(k0s):
        c0, c1 = LANES * j, LANES * (j + 1)
        a, u = _lru_gate_tile(xc_ref, xb_ref, wp_ref, ba_ref, bx_ref, sp, j, k0)
        h, carry = _scan_rows(a, u, hc_ref[:, c0:c1])
        hc_ref[:, c0:c1] = carry
        yg_ref[:, c0:c1] = (h * _gelu_tanh(gb_ref[:, c0:c1])).astype(BF16)

    @pl.when(t == pl.num_programs(1) - 1)
    def _():
        hl_ref[...] = hc_ref[...]


def _lru_seq(z, batch, seq, w_conv, b_conv, wp, b_a, b_x, lam, k0s):
    d_rnn = z.shape[1] // 2
    tt = _pick(seq, (256, 128, 64, 32, 16, 8))
    nt = seq // tt
    row = lambda b, t: b * nt + t
    full = lambda shape: pl.BlockSpec(shape, lambda b, t: (0,) * len(shape))
    yg, h_last = pl.pallas_call(
        functools.partial(_lru_seq_kernel, tt=tt, k0s=k0s),
        grid=(batch, nt),
        in_specs=[
            pl.BlockSpec((tt, d_rnn), lambda b, t: (row(b, t), 0)),
            pl.BlockSpec((tt, d_rnn), lambda b, t: (row(b, t), 1)),
            full((4, d_rnn)), full((1, d_rnn)), full(wp.shape),
            full((1, d_rnn)), full((1, d_rnn)), full((1, d_rnn)),
        ],
        out_specs=[
            pl.BlockSpec((tt, d_rnn), lambda b, t: (row(b, t), 0)),
            pl.BlockSpec((None, 1, d_rnn), lambda b, t: (b, 0, 0)),
        ],
        out_shape=[
            jax.ShapeDtypeStruct((batch * seq, d_rnn), BF16),
            jax.ShapeDtypeStruct((batch, 1, d_rnn), F32),
        ],
        scratch_shapes=[
            pltpu.VMEM((tt + SUBLANES, d_rnn), F32),
            pltpu.VMEM((tt, d_rnn), F32),
            pltpu.VMEM((tt, d_rnn), BF16),
            pltpu.VMEM((1, d_rnn), F32),
        ],
        compiler_params=_cparams(("arbitrary", "arbitrary")),
        name="lru_seq",
    )(z, z, w_conv, b_conv.reshape(1, -1), wp, b_a.reshape(1, -1), b_x.reshape(1, -1), lam.reshape(1, -1))
    return yg, h_last.reshape(batch, d_rnn)


def _lru_step_kernel(gb_ref, main_ref, p0_ref, p1_ref, p2_ref, h0_ref, wc_ref, bc_ref, wp_ref, ba_ref,
                     bx_ref, lam_ref, yg_ref, h_ref, xc_ref, xb_ref, *, k0s):
    acc = (bc_ref[...] + wc_ref[3:4, :] * main_ref[...] + wc_ref[2:3, :] * p2_ref[...]
           + wc_ref[1:2, :] * p1_ref[...] + wc_ref[0:1, :] * p0_ref[...])
    xc_ref[...] = acc
    xb_ref[...] = acc.astype(BF16)
    sp = _softplus(-lam_ref[...])
    for j, k0 in enumerate(k0s):
        c0, c1 = LANES * j, LANES * (j + 1)
        a, u = _lru_gate_tile(xc_ref, xb_ref, wp_ref, ba_ref, bx_ref, sp, j, k0)
        h = a * h0_ref[:, c0:c1] + u
        h_ref[:, c0:c1] = h
        yg_ref[:, c0:c1] = (h * _gelu_tanh(gb_ref[:, c0:c1])).astype(BF16)


def _lru_step(z, prev, h0, w_conv, b_conv, wp, b_a, b_x, lam, k0s):
    m = z.shape[0]
    d_rnn = z.shape[1] // 2
    tm = _pick(m, (128, 64, 32, 16))
    rows = lambda c: pl.BlockSpec((tm, d_rnn), lambda i: (i, c))
    full = lambda shape: pl.BlockSpec(shape, lambda i: (0,) * len(shape))
    return pl.pallas_call(
        functools.partial(_lru_step_kernel, k0s=k0s),
        grid=(m // tm,),
        in_specs=[
            rows(0), rows(1), rows(0), rows(0), rows(0), rows(0),
            full((4, d_rnn)), full((1, d_rnn)), full(wp.shape),
            full((1, d_rnn)), full((1, d_rnn)), full((1, d_rnn)),
        ],
        out_specs=[rows(0), rows(0)],
        out_shape=[jax.ShapeDtypeStruct((m, d_rnn), BF16), jax.ShapeDtypeStruct((m, d_rnn), F32)],
        scratch_shapes=[pltpu.VMEM((tm, d_rnn), F32), pltpu.VMEM((tm, d_rnn), BF16)],
        compiler_params=_cparams(("arbitrary",)),
        name="lru_step",
    )(z, z, prev[0], prev[1], prev[2], h0, w_conv, b_conv.reshape(1, -1), wp,
      b_a.reshape(1, -1), b_x.reshape(1, -1), lam.reshape(1, -1))


def _ln_silu(v, g, b):
    mu = jnp.mean(v, axis=-1, keepdims=True)
    dv = v - mu
    var = jnp.mean(dv * dv, axis=-1, keepdims=True)
    return _silu(dv * lax.rsqrt(var + EPS) * g + b)


def _conf_seq_kernel(u_ref, w_ref, b_ref, g_ref, bl_ref, o_ref, xp_ref, v_ref, *, tt, width, pad):
    d = u_ref.shape[1]

    @pl.when(pl.program_id(1) == 0)
    def _():
        xp_ref[0:pad, :] = jnp.zeros((pad, d), F32)

    xp_ref[pad:, :] = u_ref[...]
    for c in range(d // LANES):
        c0, c1 = LANES * c, LANES * (c + 1)
        v_ref[:, c0:c1] = b_ref[:, c0:c1] + _causal_taps(xp_ref[:, c0:c1], w_ref, c0, c1, tt, pad - (width - 1))
    xp_ref[0:pad, :] = xp_ref[tt:tt + pad, :]
    o_ref[...] = _ln_silu(v_ref[...], g_ref[...], bl_ref[...]).astype(BF16)


def _conf_seq(u, batch, seq, w_dw, b_dw, g_ln, b_ln):
    d = u.shape[1]
    width = w_dw.shape[0]
    pad = -(-(width - 1) // SUBLANES) * SUBLANES
    tt = _pick(seq, (128, 64, 32))
    nt = seq // tt
    full = lambda shape: pl.BlockSpec(shape, lambda b, t: (0,) * len(shape))
    return pl.pallas_call(
        functools.partial(_conf_seq_kernel, tt=tt, width=width, pad=pad),
        grid=(batch, nt),
        in_specs=[
            pl.BlockSpec((tt, d), lambda b, t: (b * nt + t, 0)),
            full((width, d)), full((1, d)), full((1, d)), full((1, d)),
        ],
        out_specs=pl.BlockSpec((tt, d), lambda b, t: (b * nt + t, 0)),
        out_shape=jax.ShapeDtypeStruct((batch * seq, d), BF16),
        scratch_shapes=[pltpu.VMEM((tt + pad, d), F32), pltpu.VMEM((tt, d), F32)],
        compiler_params=_cparams(("arbitrary", "arbitrary")),
        name="conf_seq",
    )(u, w_dw, b_dw.reshape(1, d), g_ln.reshape(1, d), b_ln.reshape(1, d))


def _conf_step_kernel(u_ref, buf_ref, w_ref, b_ref, g_ref, bl_ref, o_ref, *, width):
    acc = b_ref[...] + w_ref[width - 1:width, :] * u_ref[...]
    for k in range(width - 1):
        acc = acc + w_ref[k:k + 1, :] * buf_ref[k]
    o_ref[...] = _ln_silu(acc, g_ref[...], bl_ref[...]).astype(BF16)


def _conf_step(u, buf_t, w_dw, b_dw, g_ln, b_ln):
    m, d = u.shape
    width = w_dw.shape[0]
    tm = _pick(m, (32, 16, 8))
    full = lambda shape: pl.BlockSpec(shape, lambda i: (0,) * len(shape))
    return pl.pallas_call(
        functools.partial(_conf_step_kernel, width=width),
        grid=(m // tm,),
        in_specs=[
            pl.BlockSpec((tm, d), lambda i: (i, 0)),
            pl.BlockSpec((width - 1, tm, d), lambda i: (0, i, 0)),
            full((width, d)), full((1, d)), full((1, d)), full((1, d)),
        ],
        out_specs=pl.BlockSpec((tm, d), lambda i: (i, 0)),
        out_shape=jax.ShapeDtypeStruct((m, d), BF16),
        compiler_params=_cparams(("arbitrary",)),
        name="conf_step",
    )(u, buf_t, w_dw, b_dw.reshape(1, d), g_ln.reshape(1, d), b_ln.reshape(1, d))


def _rows_dense(rows, tn):
    sub = lax.broadcasted_iota(I32, (SUBLANES, tn), 0)
    out = jnp.zeros((SUBLANES, tn), rows[0].dtype)
    for k, r in enumerate(rows):
        out = jnp.where(sub == k, r, out)
    return out


def _expert_iota(rows, tn, first):
    return lax.broadcasted_iota(I32, (rows, tn), 0).astype(F32) + float(first)


def _router_kernel(x_ref, wrt_ref, br_ref, idx_ref, wts_ref, rank_ref, cnt_ref, carry_ref, *, tn, n_exp):
    @pl.when(pl.program_id(0) == 0)
    def _():
        carry_ref[...] = jnp.zeros_like(carry_ref)

    x = _unpack_bf16_pairs(x_ref[...])
    logits = lax.dot_general(wrt_ref[...], x, (((1,), (1,)), ((), ())), preferred_element_type=F32)
    probs = _sigmoid(logits)
    choice = probs + br_ref[...]
    row = _expert_iota(n_exp, tn, 0)
    neg = jnp.float32(-jnp.inf)
    big = jnp.float32(n_exp)
    gsz = n_exp // N_GROUPS

    score = []
    for g in range(N_GROUPS):
        v = choice[g * gsz:(g + 1) * gsz, :]
        rg = _expert_iota(gsz, tn, g * gsz)
        m1 = jnp.max(v, axis=0, keepdims=True)
        i1 = jnp.min(jnp.where(v == m1, rg, big), axis=0, keepdims=True)
        m2 = jnp.max(jnp.where(rg == i1, neg, v), axis=0, keepdims=True)
        score.append(m1 + m2)

    masked = []
    for g in range(N_GROUPS):
        beaten = jnp.zeros((1, tn), I32)
        for o in range(N_GROUPS):
            if o == g:
                continue
            wins = (score[o] >= score[g]) if o < g else (score[o] > score[g])
            beaten = beaten + wins.astype(I32)
        masked.append(jnp.where(beaten < TOPK_GROUPS, choice[g * gsz:(g + 1) * gsz, :], neg))
    masked = jnp.concatenate(masked, axis=0)

    idx, pk = [], []
    onehot = jnp.zeros((n_exp, tn), F32)
    for _ in range(TOP_K):
        m = jnp.max(masked, axis=0, keepdims=True)
        ik = jnp.min(jnp.where(masked == m, row, big), axis=0, keepdims=True)
        hit = row == ik
        pk.append(jnp.sum(jnp.where(hit, probs, 0.0), axis=0, keepdims=True))
        idx.append(ik)
        onehot = jnp.where(hit, 1.0, onehot)
        masked = jnp.where(hit, neg, masked)

    total = pk[0]
    for p in pk[1:]:
        total = total + p
    wts = [p / total * ROUTED_SCALE for p in pk]

    r_i = lax.broadcasted_iota(I32, (tn, tn), 0)
    c_i = lax.broadcasted_iota(I32, (tn, tn), 1)
    tri = jnp.where(r_i < c_i, 1.0, 0.0).astype(BF16)
    before = _dot(onehot.astype(BF16), tri) + carry_ref[...]
    rank = [jnp.sum(jnp.where(row == ik, before, 0.0), axis=0, keepdims=True).astype(I32) for ik in idx]
    carry_ref[...] = carry_ref[...] + jnp.sum(onehot, axis=1, keepdims=True)

    idx_ref[...] = _rows_dense(idx, tn).astype(I32)
    wts_ref[...] = _rows_dense(wts, tn)
    rank_ref[...] = _rows_dense(rank, tn)
    cnt_ref[...] = carry_ref[...]


def _token_tile(n):
    return _pick(n, (640, 512, 384, 256, 128))


def _router(h2p, w_router_t, b_router):
    n, dh = h2p.shape
    n_exp = w_router_t.shape[0]
    tn = _token_tile(n)
    row8 = pl.BlockSpec((TOP_K, tn), lambda i: (0, i))
    return pl.pallas_call(
        functools.partial(_router_kernel, tn=tn, n_exp=n_exp),
        grid=(n // tn,),
        in_specs=[
            pl.BlockSpec((tn, dh), lambda i: (i, 0)),
            pl.BlockSpec((n_exp, 2 * dh), lambda i: (0, 0)),
            pl.BlockSpec((n_exp, 1), lambda i: (0, 0)),
        ],
        out_specs=[row8, row8, row8, pl.BlockSpec((n_exp, 1), lambda i: (0, 0))],
        out_shape=[
            jax.ShapeDtypeStruct((TOP_K, n), I32),
            jax.ShapeDtypeStruct((TOP_K, n), F32),
            jax.ShapeDtypeStruct((TOP_K, n), I32),
            jax.ShapeDtypeStruct((n_exp, 1), F32),
        ],
        scratch_shapes=[pltpu.VMEM((n_exp, 1), F32)],
        compiler_params=_cparams(("arbitrary",)),
        name="moe_router",
    )(h2p, w_router_t, b_router.reshape(n_exp, 1))


def _dest_kernel(idx_ref, rank_ref, ps_ref, o_ref, *, tn, n_exp):
    row = lax.broadcasted_iota(I32, (n_exp, tn), 0)
    idx = idx_ref[...]
    start = []
    for k in range(TOP_K):
        hit = row == idx[k:k + 1, :]
        start.append(jnp.sum(jnp.where(hit, ps_ref[...], 0.0), axis=0, keepdims=True).astype(I32))
    o_ref[...] = _rows_dense(start, tn) + rank_ref[...]


def _dest_rows(idx, rank, pstarts):
    n = idx.shape[1]
    n_exp = pstarts.shape[0]
    tn = _token_tile(n)
    row8 = pl.BlockSpec((TOP_K, tn), lambda i: (0, i))
    return pl.pallas_call(
        functools.partial(_dest_kernel, tn=tn, n_exp=n_exp),
        grid=(n // tn,),
        in_specs=[row8, row8, pl.BlockSpec((n_exp, 1), lambda i: (0, 0))],
        out_specs=row8,
        out_shape=jax.ShapeDtypeStruct((TOP_K, n), I32),
        compiler_params=_cparams(("arbitrary",)),
        name="moe_dest",
    )(idx, rank, pstarts)


def _dispatch_kernel(dest_ref, x_ref, wsg_ref, wsu_ref, wsd_ref, xs_hbm, sh_ref, wgb_ref, wub_ref, wdb_ref,
                     sem, *, tm):
    @pl.when(pl.program_id(0) == 0)
    def _():
        wgb_ref[...] = wsg_ref[...].astype(BF16)
        wub_ref[...] = wsu_ref[...].astype(BF16)
        wdb_ref[...] = wsd_ref[...].astype(BF16)

    def scatter_rows(group):
        for t in range(group * tm // DISPATCH_GROUPS, (group + 1) * tm // DISPATCH_GROUPS):
            for k in range(TOP_K):
                d = dest_ref[0, t * TOP_K + k]
                pltpu.make_async_copy(x_ref.at[pl.ds(t, 1)], xs_hbm.at[pl.ds(d, 1)], sem).start(priority=k % 2)

    x = _unpack_bf16_pairs(x_ref[...])
    de, d = wdb_ref.shape
    group = 0
    halves = []
    for w_ref in (wgb_ref, wub_ref):
        parts = []
        for c in range(2):
            scatter_rows(group)
            group += 1
            parts.append(_dot(x, w_ref[:, c * de // 2:(c + 1) * de // 2]))
        halves.append(jnp.concatenate(parts, axis=1))
    act = (_silu(halves[0]) * halves[1]).astype(BF16)
    for c in range(4):
        scatter_rows(group)
        group += 1
        sh_ref[:, c * d // 4:(c + 1) * d // 4] = _dot(act, wdb_ref[:, c * d // 4:(c + 1) * d // 4])
    assert group == DISPATCH_GROUPS

    for k in range(TOP_K):
        pltpu.make_async_copy(x_ref, xs_hbm.at[pl.ds(0, tm)], sem).wait()


def _dispatch_shared(h2p, dest, w_sg, w_su, w_sd, layer, n_rows):
    n, dh = h2p.shape
    d = 2 * dh
    de = w_sg.shape[2]
    tm = _pick(n, (320, 256, 176, 128, 64, 48, 32, 16))
    nt = n // tm
    xs, shared = pl.pallas_call(
        functools.partial(_dispatch_kernel, tm=tm),
        grid=(nt,),
        in_specs=[
            pl.BlockSpec((None, 1, tm * TOP_K), lambda i: (i, 0, 0), memory_space=pltpu.SMEM),
            pl.BlockSpec((tm, dh), lambda i: (i, 0)),
            pl.BlockSpec((None, d, de), lambda i: (layer, 0, 0)),
            pl.BlockSpec((None, d, de), lambda i: (layer, 0, 0)),
            pl.BlockSpec((None, de, d), lambda i: (layer, 0, 0)),
        ],
        out_specs=[
            pl.BlockSpec(memory_space=pl.ANY),
            pl.BlockSpec((tm, d), lambda i: (i, 0)),
        ],
        out_shape=[
            jax.ShapeDtypeStruct((n_rows, dh), U32),
            jax.ShapeDtypeStruct((n, d), F32),
        ],
        scratch_shapes=[
            pltpu.VMEM((d, de), BF16), pltpu.VMEM((d, de), BF16), pltpu.VMEM((de, d), BF16),
            pltpu.SemaphoreType.DMA(()),
        ],
        compiler_params=_cparams(("arbitrary",)),
        name="moe_dispatch",
    )(dest.reshape(nt, 1, tm * TOP_K), h2p, w_sg, w_su, w_sd)
    return xs, shared


def _expert_kernel(be_ref, bv_ref, na_ref, sz_ref, xs_ref, wg_hbm, wu_hbm, wd_hbm, y_ref,
                   wg_buf, wu_buf, wd_buf, wgb_ref, wub_ref, wdb_ref, sem, slot_ref, *, layer, n_exp):
    i = pl.program_id(0)
    e = be_ref[i]

    def weight_copies(expert, slot):
        copies = []
        for w, (hbm, buf) in enumerate(((wg_hbm, wg_buf), (wu_hbm, wu_buf), (wd_hbm, wd_buf))):
            rows = buf.shape[1] // WEIGHT_DMA_CHUNKS
            for c in range(WEIGHT_DMA_CHUNKS):
                part = pl.ds(c * rows, rows)
                copies.append(pltpu.make_async_copy(hbm.at[layer, expert, part], buf.at[slot, part], sem.at[slot, w]))
        return copies

    def start_weights(expert, slot):
        for n, c in enumerate(weight_copies(expert, slot)):
            c.start(priority=n % 2)

    @pl.when(i == 0)
    def _():
        slot_ref[0] = 0
        start_weights(e, 0)

    is_first = (i == 0) | (e != be_ref[jnp.maximum(i - 1, 0)])

    @pl.when(is_first & (i < na_ref[0]))
    def _():
        slot = slot_ref[0]
        nxt = lax.while_loop(lambda n: (n < n_exp) & (sz_ref[jnp.minimum(n, n_exp - 1)] == 0),
                             lambda n: n + 1, e + 1)

        @pl.when(nxt < n_exp)
        def _():
            start_weights(nxt, 1 - slot)

        for c in weight_copies(e, slot):
            c.wait()
        wgb_ref[...] = wg_buf[slot].astype(BF16)
        wub_ref[...] = wu_buf[slot].astype(BF16)
        wdb_ref[...] = wd_buf[slot].astype(BF16)
        slot_ref[0] = 1 - slot

    @pl.when(i < na_ref[0])
    def _():
        x = _unpack_bf16_pairs(xs_ref[...])
        row = lax.broadcasted_iota(I32, x.shape, 0)
        x = jnp.where(row < bv_ref[i], x, jnp.zeros_like(x))
        act = _silu(_dot(x, wgb_ref[...])) * _dot(x, wub_ref[...])
        y_ref[...] = _dot(act.astype(BF16), wdb_ref[...])


def _experts(xs, blk_exp, blk_valid, n_active, sizes, w_eg, w_eu, w_ed, layer):
    n_rows, dh = xs.shape
    d = 2 * dh
    n_exp, _, de = w_eg.shape[1:]
    nb = n_rows // ROW_BLOCK
    blk = lambda i, be, bv, na, sz: (jnp.minimum(i, na[0] - 1), 0)
    grid_spec = pltpu.PrefetchScalarGridSpec(
        num_scalar_prefetch=4,
        grid=(nb,),
        in_specs=[
            pl.BlockSpec((ROW_BLOCK, dh), blk),
            pl.BlockSpec(memory_space=pl.ANY),
            pl.BlockSpec(memory_space=pl.ANY),
            pl.BlockSpec(memory_space=pl.ANY),
        ],
        out_specs=pl.BlockSpec((ROW_BLOCK, d), blk),
        scratch_shapes=[
            pltpu.VMEM((2, d, de), F32), pltpu.VMEM((2, d, de), F32), pltpu.VMEM((2, de, d), F32),
            pltpu.VMEM((d, de), BF16), pltpu.VMEM((d, de), BF16), pltpu.VMEM((de, d), BF16),
            pltpu.SemaphoreType.DMA((2, 3)),
            pltpu.SMEM((1,), I32),
        ],
    )
    return pl.pallas_call(
        functools.partial(_expert_kernel, layer=layer, n_exp=n_exp),
        grid_spec=grid_spec,
        out_shape=jax.ShapeDtypeStruct((n_rows, d), F32),
        compiler_params=_cparams(("arbitrary",)),
        name="moe_experts",
    )(blk_exp, blk_valid, n_active, sizes, xs, w_eg, w_eu, w_ed)


COMBINE_ROWS = 16
WEIGHT_DMA_CHUNKS = 4
DISPATCH_GROUPS = 8


def _combine_kernel(dest_ref, dnext_ref, y_hbm, wts_ref, sh_ref, x_ref, gate_ref, *rest, tm, post):
    if post == "mix":
        g_ref, shift_ref, scale_ref, o_ref, h_ref, gbuf, sem = rest
    elif post == "final":
        g_ref, h_ref, gbuf, sem = rest
    else:
        o_ref, gbuf, sem = rest
    i = pl.program_id(0)
    slot = lax.rem(i, 2)

    def start_row(dref, s, t):
        for k in range(TOP_K):
            pltpu.make_async_copy(y_hbm.at[pl.ds(dref[0, t * TOP_K + k], 1)], gbuf.at[s, k, pl.ds(t, 1)],
                                  sem.at[s]).start(priority=k % 2)

    def wait_slot(s):
        for k in range(TOP_K):
            pltpu.make_async_copy(y_hbm.at[pl.ds(0, tm)], gbuf.at[s, k], sem.at[s]).wait()

    @pl.when(i == 0)
    def _():
        def body(t, carry):
            start_row(dest_ref, 0, t)
            return carry

        lax.fori_loop(0, tm, body, 0)

    wait_slot(slot)

    def rows_of(ref, rows):
        return ref[...] if ref.shape[0] == 1 else ref[rows, :]

    for q in range(tm // COMBINE_ROWS):
        for t in range(q * COMBINE_ROWS, (q + 1) * COMBINE_ROWS):
            start_row(dnext_ref, 1 - slot, t)
        rows = slice(q * COMBINE_ROWS, (q + 1) * COMBINE_ROWS)
        w = wts_ref[rows, :]
        f = sh_ref[rows, :]
        for k in range(TOP_K):
            f = f + w[:, k:k + 1] * gbuf[slot, k, rows, :]
        xn = x_ref[rows, :] + rows_of(gate_ref, rows) * f
        if post != "final":
            o_ref[rows, :] = xn
        if post is not None:
            y = xn * lax.rsqrt(jnp.mean(xn * xn, axis=-1, keepdims=True) + EPS) * g_ref[...]
            if post == "mix":
                y = y * (1.0 + rows_of(scale_ref, rows)) + rows_of(shift_ref, rows)
            h_ref[rows, :] = y.astype(h_ref.dtype)

    @pl.when(i == pl.num_programs(0) - 1)
    def _():
        wait_slot(1 - slot)


def _combine(y_rows, dest, wts, shared, x, gate, seq, row_off, post=None, g=None, shift=None, scale=None):
    m, d = x.shape
    tm = _pick(m if seq == 1 else seq, (128, 64, 32, 16))
    off = row_off // tm
    nt = m // tm
    dest3 = dest[row_off:row_off + m].reshape(nt, 1, tm * TOP_K)
    smem = lambda nxt: pl.BlockSpec((None, 1, tm * TOP_K), lambda i: (jnp.minimum(i + nxt, nt - 1), 0, 0),
                                    memory_space=pltpu.SMEM)
    tile = pl.BlockSpec((tm, d), lambda i: (i, 0))
    in_specs = [
        smem(0), smem(1),
        pl.BlockSpec(memory_space=pl.ANY),
        pl.BlockSpec((tm, TOP_K), lambda i: (i + off, 0)),
        pl.BlockSpec((tm, d), lambda i: (i + off, 0)),
        tile,
        _mod_spec(seq, tm, d, False),
    ]
    args = [dest3, dest3, y_rows, wts, shared, x, gate]
    x_out = jax.ShapeDtypeStruct((m, d), F32)
    if post == "mix":
        in_specs += [pl.BlockSpec((1, d), lambda i: (0, 0)), _mod_spec(seq, tm, d, False), _mod_spec(seq, tm, d, False)]
        args += [g.reshape(1, d), shift, scale]
        out_specs, out_shape = [tile, tile], [x_out, jax.ShapeDtypeStruct((m, d), BF16)]
    elif post == "final":
        in_specs += [pl.BlockSpec((1, d), lambda i: (0, 0))]
        args += [g.reshape(1, d)]
        out_specs, out_shape = tile, x_out
    else:
        out_specs, out_shape = tile, x_out
    return pl.pallas_call(
        functools.partial(_combine_kernel, tm=tm, post=post),
        grid=(nt,),
        in_specs=in_specs,
        out_specs=out_specs,
        out_shape=out_shape,
        scratch_shapes=[pltpu.VMEM((2, TOP_K, tm, d), F32), pltpu.SemaphoreType.DMA((2,))],
        compiler_params=_cparams(("arbitrary",)),
        name="moe_combine",
    )(*args)


def _moe(h2p, w_router_b, b_router, w_eg, w_eu, w_ed, w_sg, w_su, w_sd, layer):
    n = h2p.shape[0]
    n_exp = w_router_b.shape[0]
    idx, wts, rank, counts = _router(h2p, w_router_b, b_router)

    sizes = counts[:, 0].astype(I32)
    padded = (sizes + ROW_BLOCK - 1) // ROW_BLOCK * ROW_BLOCK
    pends = jnp.cumsum(padded)
    pstarts = pends - padded
    n_blocks = -(-(n * TOP_K) // ROW_BLOCK) + n_exp
    n_active = (pends[-1] // ROW_BLOCK).astype(I32)
    experts = jnp.arange(n_exp, dtype=I32)
    last_exp = jnp.max(jnp.where(sizes > 0, experts, 0))
    blk_start = jnp.arange(n_blocks, dtype=I32) * ROW_BLOCK
    blk_exp = jnp.minimum(jnp.sum(pends[None, :] <= blk_start[:, None], axis=1).astype(I32), last_exp)
    seg_end = jnp.sum(jnp.where(blk_exp[:, None] == experts[None, :], (pstarts + sizes)[None, :], 0), axis=1)
    blk_valid = jnp.clip(seg_end - blk_start, 0, ROW_BLOCK).astype(I32)
    dest = _dest_rows(idx, rank, pstarts.astype(F32).reshape(n_exp, 1)).T

    xs, shared = _dispatch_shared(h2p, dest, w_sg, w_su, w_sd, layer, n_blocks * ROW_BLOCK)
    y_rows = _experts(xs, blk_exp, blk_valid, n_active.reshape(1), sizes, w_eg, w_eu, w_ed, layer)
    return y_rows, dest, wts.T, shared


def _last_rows(x, batch, seq, n):
    return jnp.stack([x[(b + 1) * seq - n:(b + 1) * seq] for b in range(batch)])


def _group_mods(mod_l, b0, batch, seq, d):
    m = mod_l[b0:b0 + batch].reshape(batch, N_MOD, d)
    if seq > 1:
        return [m[:, k, :].reshape(batch, 1, d) for k in range(N_MOD)]
    return [m[:, k, :].reshape(1, batch, d) for k in range(N_MOD)]


def kernel(x_prompt, x_sample, state_lru_h, state_lru_conv, state_conv, c_prompt, c_sample, w_ada, b_ada, g_mix, g_ffn, w_lru_in, w_lru_conv, b_lru_conv, w_lru_a, b_lru_a, w_lru_x, b_lru_x, lru_lambda, w_lru_out, w_pw1, b_pw1, w_dw, b_dw, g_cln, b_cln, w_pw2, b_pw2, w_router, b_router, w_exp_gate, w_exp_up, w_exp_down, w_sh_gate, w_sh_up, w_sh_down, g_final):
    bp, tp, d = x_prompt.shape
    bs, ts, _ = x_sample.shape
    assert ts == 1
    depth = w_ada.shape[0]
    d_rnn = w_lru_in.shape[2] // 2
    np_, ns = bp * tp, bs * ts

    c_all = jnp.concatenate([c_prompt, c_sample], axis=0)
    pad_rows = -c_all.shape[0] % SUBLANES
    c_all = jnp.pad(c_all, ((0, pad_rows), (0, 0)))
    mod = _ada_mod(c_all, w_ada, b_ada)

    xp = x_prompt.reshape(np_, d)
    xs = x_sample.reshape(ns, d)
    k0s = _gate_windows(d_rnn, d_rnn // N_LRU_BLOCKS)
    zero_b = jnp.zeros((d,), F32)

    hs_p, hs_s, lcs_p, lcs_s, cbs_p, cbs_s = [], [], [], [], [], []
    all_mods = [(_group_mods(mod[l], 0, bp, tp, d), _group_mods(mod[l], bp, bs, ts, d)) for l in range(depth)]
    hp = _norm_mod(xp, g_mix[0], all_mods[0][0][0], all_mods[0][0][1], tp, False)
    hsm = _norm_mod(xs, g_mix[0], all_mods[0][1][0], all_mods[0][1][1], ts, False)
    for l in range(depth):
        mods_p, mods_s = all_mods[l]
        j = l // 2
        if l % 2 == 0:
            wp = _pack_gate_weights(w_lru_a[j], w_lru_x[j])
            lru_args = (w_lru_conv[j], b_lru_conv[j], wp, b_lru_a[j], b_lru_x[j], lru_lambda[j], k0s)
            zp = _mm_plain(hp, w_lru_in, j, (768, 896, 512, 384, 256, 128))
            zs = _mm_plain(hsm, w_lru_in, j, (768, 896, 512, 384, 256, 128))
            ygp, hlp = _lru_seq(zp, bp, tp, *lru_args)
            buf = state_lru_conv[j]
            ygs, hls = _lru_step(zs, (buf[:, 0], buf[:, 1], buf[:, 2]), state_lru_h[j], *lru_args)
            hs_p.append(hlp)
            hs_s.append(hls)
            lcs_p.append(_last_rows(zp[:, d_rnn:], bp, tp, 3))
            lcs_s.append(jnp.concatenate([buf[:, 1:], zs[:, None, d_rnn:]], axis=1))
            xp = _mm_res(ygp, w_lru_out, zero_b, xp, mods_p[2], j, tp)
            xs = _mm_res(ygs, w_lru_out, zero_b, xs, mods_s[2], j, ts)
        else:
            up = _mm_glu(hp, w_pw1, b_pw1, j)
            us = _mm_glu(hsm, w_pw1, b_pw1, j)
            conf_args = (w_dw[j], b_dw[j], g_cln[j], b_cln[j])
            vp = _conf_seq(up, bp, tp, *conf_args)
            buf = state_conv[j]
            vs = _conf_step(us, jnp.transpose(buf, (1, 0, 2)), *conf_args)
            width = w_dw.shape[1]
            cbs_p.append(_last_rows(up, bp, tp, width - 1))
            cbs_s.append(jnp.concatenate([buf[:, 1:], us[:, None, :]], axis=1))
            xp = _mm_res(vp, w_pw2, b_pw2[j], xp, mods_p[2], j, tp)
            xs = _mm_res(vs, w_pw2, b_pw2[j], xs, mods_s[2], j, ts)

        h2p = jnp.concatenate([
            _norm_mod(xp, g_ffn[l], mods_p[3], mods_p[4], tp, True),
            _norm_mod(xs, g_ffn[l], mods_s[3], mods_s[4], ts, True),
        ], axis=0)
        y_rows, dest, wts, shared = _moe(h2p, w_router[l].T.astype(BF16), b_router[l], w_exp_gate, w_exp_up,
                                         w_exp_down, w_sh_gate, w_sh_up, w_sh_down, l)
        moe_p = (y_rows, dest, wts, shared, xp, mods_p[5], tp, 0)
        moe_s = (y_rows, dest, wts, shared, xs, mods_s[5], ts, np_)
        if l + 1 < depth:
            nxt_p, nxt_s = all_mods[l + 1]
            xp, hp = _combine(*moe_p, post="mix", g=g_mix[l + 1], shift=nxt_p[0], scale=nxt_p[1])
            xs, hsm = _combine(*moe_s, post="mix", g=g_mix[l + 1], shift=nxt_s[0], scale=nxt_s[1])
        else:
            y_prompt = _combine(*moe_p, post="final", g=g_final).reshape(bp, tp, d)
            y_sample = _combine(*moe_s, post="final", g=g_final).reshape(bs, ts, d)

    return (y_prompt, y_sample, jnp.stack(hs_p), jnp.stack(lcs_p), jnp.stack(cbs_p),
            jnp.stack(hs_s), jnp.stack(lcs_s), jnp.stack(cbs_s))
```

```python
import functools

import jax
import jax.numpy as jnp
from jax import lax
from jax.experimental import pallas as pl
from jax.experimental.pallas import tpu as pltpu

F32 = jnp.float32
BF16 = jnp.bfloat16
I32 = jnp.int32
U32 = jnp.uint32

N_LRU_BLOCKS = 16
LRU_C = 8.0
TOP_K = 8
N_GROUPS = 8
TOPK_GROUPS = 4
ROUTED_SCALE = 2.5
ROW_BLOCK = 128
N_MOD = 6
EPS = 1e-6

LANES = 128
SUBLANES = 8
GATE_WINDOW = 512
V7X_VMEM_BYTES = 64 * 1024 * 1024
VMEM_LIMIT = V7X_VMEM_BYTES - 8 * 1024 * 1024


def _pick(n, prefs):
    for p in prefs:
        if n % p == 0:
            return p
    return n


def _cparams(sem):
    return pltpu.CompilerParams(dimension_semantics=sem, vmem_limit_bytes=VMEM_LIMIT)


def _dot(a, b):
    return jnp.dot(a, b, preferred_element_type=F32)


def _sigmoid(x):
    return jax.nn.sigmoid(x)


def _silu(x):
    return x * jax.nn.sigmoid(x)


def _pack_bf16_pairs(h):
    half = h.shape[1] // 2
    bits = lax.bitcast_convert_type(h.astype(BF16).astype(F32), U32)
    return (bits[:, :half] >> 16) | (bits[:, half:] & jnp.uint32(0xFFFF0000))


def _unpack_bf16_pairs(w):
    lo = lax.bitcast_convert_type(w << 16, F32).astype(BF16)
    hi = lax.bitcast_convert_type(w & jnp.uint32(0xFFFF0000), F32).astype(BF16)
    return jnp.concatenate([lo, hi], axis=1)


def _ada_kernel(c_ref, w_ref, b_ref, o_ref):
    a = _silu(c_ref[...]).astype(BF16)
    o_ref[...] = _dot(a, w_ref[...].astype(BF16)) + b_ref[...]


def _ada_mod(c_all, w_ada, b_ada):
    depth, d, n = w_ada.shape
    bp = c_all.shape[0]
    tn = _pick(n, (1536, 1024, 512, 256, 128))
    return pl.pallas_call(
        _ada_kernel,
        grid=(depth, n // tn),
        in_specs=[
            pl.BlockSpec((bp, d), lambda l, j: (0, 0)),
            pl.BlockSpec((None, d, tn), lambda l, j: (l, 0, j)),
            pl.BlockSpec((None, 1, tn), lambda l, j: (l, 0, j)),
        ],
        out_specs=pl.BlockSpec((None, bp, tn), lambda l, j: (l, 0, j)),
        out_shape=jax.ShapeDtypeStruct((depth, bp, n), F32),
        compiler_params=_cparams(("arbitrary", "arbitrary")),
        name="ada_mod",
    )(c_all, w_ada, b_ada.reshape(depth, 1, n))


def _mod_spec(seq, tm, tn, col_axis_first):
    if seq > 1:
        shape = (None, 1, tn)
        if col_axis_first:
            return pl.BlockSpec(shape, lambda n, i: ((i * tm) // seq, 0, n))
        return pl.BlockSpec(shape, lambda i: ((i * tm) // seq, 0, 0))
    shape = (None, tm, tn)
    if col_axis_first:
        return pl.BlockSpec(shape, lambda n, i: (0, i, n))
    return pl.BlockSpec(shape, lambda i: (0, i, 0))


def _norm_mod_kernel(x_ref, g_ref, sh_ref, sc_ref, o_ref, *, pack):
    x = x_ref[...]
    y = x * lax.rsqrt(jnp.mean(x * x, axis=-1, keepdims=True) + EPS)
    h = (y * g_ref[...]) * (1.0 + sc_ref[...]) + sh_ref[...]
    if pack:
        o_ref[...] = _pack_bf16_pairs(h)
    else:
        o_ref[...] = h.astype(BF16)


def _norm_mod(x, g, shift, scale, seq, pack):
    m, d = x.shape
    tm = _pick(m if seq == 1 else seq, (512, 256, 128, 64, 32, 16, 8))
    out_d, out_t = (d // 2, U32) if pack else (d, BF16)
    return pl.pallas_call(
        functools.partial(_norm_mod_kernel, pack=pack),
        grid=(m // tm,),
        in_specs=[
            pl.BlockSpec((tm, d), lambda i: (i, 0)),
            pl.BlockSpec((1, d), lambda i: (0, 0)),
            _mod_spec(seq, tm, d, False),
            _mod_spec(seq, tm, d, False),
        ],
        out_specs=pl.BlockSpec((tm, out_d), lambda i: (i, 0)),
        out_shape=jax.ShapeDtypeStruct((m, out_d), out_t),
        compiler_params=_cparams(("arbitrary",)),
        name="norm_mod",
    )(x, g.reshape(1, d), shift, scale)


def _mm_plain_kernel(a_ref, w_ref, o_ref, wb_ref):
    @pl.when(pl.program_id(1) == 0)
    def _():
        wb_ref[...] = w_ref[...].astype(BF16)

    o_ref[...] = _dot(a_ref[...], wb_ref[...])


def _mm_plain(a, w, layer, tn_prefs):
    m, k = a.shape
    n = w.shape[2]
    tm = _pick(m, (1024, 512, 256, 128, 64, 32, 16))
    tn = _pick(n, tn_prefs)
    return pl.pallas_call(
        _mm_plain_kernel,
        grid=(n // tn, m // tm),
        in_specs=[
            pl.BlockSpec((tm, k), lambda j, i: (i, 0)),
            pl.BlockSpec((None, k, tn), lambda j, i: (layer, 0, j)),
        ],
        out_specs=pl.BlockSpec((tm, tn), lambda j, i: (i, j)),
        out_shape=jax.ShapeDtypeStruct((m, n), F32),
        scratch_shapes=[pltpu.VMEM((k, tn), BF16)],
        compiler_params=_cparams(("arbitrary", "arbitrary")),
        name="mm_plain",
    )(a, w)


def _mm_glu_kernel(a_ref, wa_ref, wg_ref, ba_ref, bg_ref, o_ref, wab_ref, wgb_ref):
    @pl.when(pl.program_id(1) == 0)
    def _():
        wab_ref[...] = wa_ref[...].astype(BF16)
        wgb_ref[...] = wg_ref[...].astype(BF16)

    a = a_ref[...]
    va = _dot(a, wab_ref[...]) + ba_ref[...]
    vg = _dot(a, wgb_ref[...]) + bg_ref[...]
    o_ref[...] = va * _sigmoid(vg)


def _mm_glu(a, w, b, layer):
    m, k = a.shape
    n = w.shape[2] // 2
    tm = _pick(m, (1024, 512, 256, 128, 64, 32, 16))
    tn = _pick(n, (512, 256, 128))
    nj = n // tn
    b3 = b.reshape(b.shape[0], 1, 2 * n)
    return pl.pallas_call(
        _mm_glu_kernel,
        grid=(nj, m // tm),
        in_specs=[
            pl.BlockSpec((tm, k), lambda j, i: (i, 0)),
            pl.BlockSpec((None, k, tn), lambda j, i: (layer, 0, j)),
            pl.BlockSpec((None, k, tn), lambda j, i: (layer, 0, j + nj)),
            pl.BlockSpec((None, 1, tn), lambda j, i: (layer, 0, j)),
            pl.BlockSpec((None, 1, tn), lambda j, i: (layer, 0, j + nj)),
        ],
        out_specs=pl.BlockSpec((tm, tn), lambda j, i: (i, j)),
        out_shape=jax.ShapeDtypeStruct((m, n), F32),
        scratch_shapes=[pltpu.VMEM((k, tn), BF16), pltpu.VMEM((k, tn), BF16)],
        compiler_params=_cparams(("arbitrary", "arbitrary")),
        name="mm_glu",
    )(a, w, w, b3, b3)


def _mm_res_kernel(a_ref, w_ref, b_ref, x_ref, gate_ref, o_ref, wb_ref):
    @pl.when(pl.program_id(1) == 0)
    def _():
        wb_ref[...] = w_ref[...].astype(BF16)

    out = _dot(a_ref[...], wb_ref[...]) + b_ref[...]
    o_ref[...] = x_ref[...] + gate_ref[...] * out


def _mm_res(a, w, b, x, gate, layer, seq):
    m, k = a.shape
    n = w.shape[2]
    tm = _pick(m if seq == 1 else seq, (512, 256, 128, 64, 32, 16))
    tn = _pick(n, (1024, 512, 256, 128))
    return pl.pallas_call(
        _mm_res_kernel,
        grid=(n // tn, m // tm),
        in_specs=[
            pl.BlockSpec((tm, k), lambda j, i: (i, 0)),
            pl.BlockSpec((None, k, tn), lambda j, i: (layer, 0, j)),
            pl.BlockSpec((1, tn), lambda j, i: (0, j)),
            pl.BlockSpec((tm, tn), lambda j, i: (i, j)),
            _mod_spec(seq, tm, tn, True),
        ],
        out_specs=pl.BlockSpec((tm, tn), lambda j, i: (i, j)),
        out_shape=jax.ShapeDtypeStruct((m, n), F32),
        scratch_shapes=[pltpu.VMEM((k, tn), BF16)],
        compiler_params=_cparams(("arbitrary", "arbitrary")),
        name="mm_res",
    )(a, w, b.reshape(1, n), x, gate)


def _gate_windows(d_rnn, bs):
    k0s = []
    for j in range(d_rnn // LANES):
        c0, c1 = LANES * j, LANES * (j + 1)
        n_lo, n_hi = c0 // bs, (c1 - 1) // bs
        k0 = min((n_lo * bs) // LANES * LANES, d_rnn - GATE_WINDOW)
        assert k0 <= n_lo * bs and (n_hi + 1) * bs <= k0 + GATE_WINDOW
        k0s.append(k0)
    return k0s


def _pack_gate_weights(w_a, w_x):
    nb, bs, _ = w_a.shape
    d_rnn = nb * bs
    blk = jnp.arange(d_rnn, dtype=I32) // bs
    on_diag = blk[:, None] == blk[None, :]

    def dense(w):
        return jnp.where(on_diag, jnp.tile(w.reshape(d_rnn, bs), (1, nb)), 0.0)

    fa, fx = dense(w_a), dense(w_x)
    tiles = []
    for j, k0 in enumerate(_gate_windows(d_rnn, bs)):
        c0 = LANES * j
        tiles.append(jnp.concatenate([fa[k0:k0 + GATE_WINDOW, c0:c0 + LANES],
                                      fx[k0:k0 + GATE_WINDOW, c0:c0 + LANES]], axis=1))
    return jnp.stack(tiles).astype(BF16)


def _softplus(x):
    return jnp.maximum(x, 0.0) + jnp.log1p(jnp.exp(-jnp.abs(x)))


def _gelu_tanh(x):
    c = 0.7978845608028654
    return 0.5 * x * (1.0 + jnp.tanh(c * (x + 0.044715 * (x * x * x))))


def _causal_taps(x, w_ref, c0, c1, rows, first):
    n = x.shape[0]
    acc = None
    for r in range(SUBLANES):
        taps = [k for k in range(w_ref.shape[0]) if (first + k) % SUBLANES == r]
        if not taps:
            continue
        xr = x if r == 0 else pltpu.roll(x, n - r, 0)
        for k in taps:
            q = first + k - r
            term = w_ref[k:k + 1, c0:c1] * xr[q:q + rows, :]
            acc = term if acc is None else acc + term
    return acc


def _lru_gate_tile(xc, xb, wp_ref, ba_ref, bx_ref, sp, j, k0):
    c0, c1 = LANES * j, LANES * (j + 1)
    pre = _dot(xb[:, k0:k0 + GATE_WINDOW], wp_ref[j])
    r = _sigmoid(pre[:, :LANES] + ba_ref[:, c0:c1])
    i = _sigmoid(pre[:, LANES:] + bx_ref[:, c0:c1])
    log_a = (-LRU_C * r) * sp[:, c0:c1]
    a = jnp.exp(log_a)
    u = jnp.sqrt(1.0 - a * a) * (i * xc[:, c0:c1])
    return a, u


def _scan_rows(a, u, carry):
    rows = a.shape[0]
    groups = rows // SUBLANES
    a = a.reshape(groups, SUBLANES, LANES)
    u = u.reshape(groups, SUBLANES, LANES)
    sub = lax.broadcasted_iota(I32, a.shape, 1)
    d = 1
    while d < SUBLANES:
        keep = sub >= d
        a_s = jnp.where(keep, pltpu.roll(a, d, 1), 1.0)
        u_s = jnp.where(keep, pltpu.roll(u, d, 1), 0.0)
        u = a * u_s + u
        a = a * a_s
        d *= 2
    hs = []
    for g in range(groups):
        hg = a[g] * carry + u[g]
        hs.append(hg)
        carry = hg[SUBLANES - 1:SUBLANES, :]
    return jnp.concatenate(hs, axis=0), carry


def _lru_seq_kernel(gb_ref, main_ref, wc_ref, bc_ref, wp_ref, ba_ref, bx_ref, lam_ref,
                    yg_ref, hl_ref, xp_ref, xc_ref, xb_ref, hc_ref, *, tt, k0s):
    t = pl.program_id(1)
    d_rnn = main_ref.shape[1]

    @pl.when(t == 0)
    def _():
        xp_ref[0:SUBLANES, :] = jnp.zeros((SUBLANES, d_rnn), F32)
        hc_ref[...] = jnp.zeros_like(hc_ref)

    xp_ref[SUBLANES:, :] = main_ref[...]
    width = wc_ref.shape[0]
    for j in range(d_rnn // LANES):
        c0, c1 = LANES * j, LANES * (j + 1)
        acc = bc_ref[:, c0:c1] + _causal_taps(xp_ref[:, c0:c1], wc_ref, c0, c1, tt, SUBLANES - (width - 1))
        xc_ref[:, c0:c1] = acc
        xb_ref[:, c0:c1] = acc.astype(BF16)
    xp_ref[0:SUBLANES, :] = xp_ref[tt:tt + SUBLANES, :]

    sp = _softplus(-lam_ref[...])
    for j, k0 in enumerate(k0s):
        c0, c1 = LANES * j, LANES * (j + 1)
        a, u = _lru_gate_tile(xc_ref, xb_ref, wp_ref, ba_ref, bx_ref, sp, j, k0)
        h, carry = _scan_rows(a, u, hc_ref[:, c0:c1])
        hc_ref[:, c0:c1] = carry
        yg_ref[:, c0:c1] = (h * _gelu_tanh(gb_ref[:, c0:c1])).astype(BF16)

    @pl.when(t == pl.num_programs(1) - 1)
    def _():
        hl_ref[...] = hc_ref[...]


def _lru_seq(z, batch, seq, w_conv, b_conv, wp, b_a, b_x, lam, k0s):
    d_rnn = z.shape[1] // 2
    tt = _pick(seq, (256, 128, 64, 32, 16, 8))
    nt = seq // tt
    row = lambda b, t: b * nt + t
    full = lambda shape: pl.BlockSpec(shape, lambda b, t: (0,) * len(shape))
    yg, h_last = pl.pallas_call(
        functools.partial(_lru_seq_kernel, tt=tt, k0s=k0s),
        grid=(batch, nt),
        in_specs=[
            pl.BlockSpec((tt, d_rnn), lambda b, t: (row(b, t), 0)),
            pl.BlockSpec((tt, d_rnn), lambda b, t: (row(b, t), 1)),
            full((4, d_rnn)), full((1, d_rnn)), full(wp.shape),
            full((1, d_rnn)), full((1, d_rnn)), full((1, d_rnn)),
        ],
        out_specs=[
            pl.BlockSpec((tt, d_rnn), lambda b, t: (row(b, t), 0)),
            pl.BlockSpec((None, 1, d_rnn), lambda b, t: (b, 0, 0)),
        ],
        out_shape=[
            jax.ShapeDtypeStruct((batch * seq, d_rnn), BF16),
            jax.ShapeDtypeStruct((batch, 1, d_rnn), F32),
        ],
        scratch_shapes=[
            pltpu.VMEM((tt + SUBLANES, d_rnn), F32),
            pltpu.VMEM((tt, d_rnn), F32),
            pltpu.VMEM((tt, d_rnn), BF16),
            pltpu.VMEM((1, d_rnn), F32),
        ],
        compiler_params=_cparams(("arbitrary", "arbitrary")),
        name="lru_seq",
    )(z, z, w_conv, b_conv.reshape(1, -1), wp, b_a.reshape(1, -1), b_x.reshape(1, -1), lam.reshape(1, -1))
    return yg, h_last.reshape(batch, d_rnn)


def _lru_step_kernel(gb_ref, main_ref, p0_ref, p1_ref, p2_ref, h0_ref, wc_ref, bc_ref, wp_ref, ba_ref,
                     bx_ref, lam_ref, yg_ref, h_ref, xc_ref, xb_ref, *, k0s):
    acc = (bc_ref[...] + wc_ref[3:4, :] * main_ref[...] + wc_ref[2:3, :] * p2_ref[...]
           + wc_ref[1:2, :] * p1_ref[...] + wc_ref[0:1, :] * p0_ref[...])
    xc_ref[...] = acc
    xb_ref[...] = acc.astype(BF16)
    sp = _softplus(-lam_ref[...])
    for j, k0 in enumerate(k0s):
        c0, c1 = LANES * j, LANES * (j + 1)
        a, u = _lru_gate_tile(xc_ref, xb_ref, wp_ref, ba_ref, bx_ref, sp, j, k0)
        h = a * h0_ref[:, c0:c1] + u
        h_ref[:, c0:c1] = h
        yg_ref[:, c0:c1] = (h * _gelu_tanh(gb_ref[:, c0:c1])).astype(BF16)


def _lru_step(z, prev, h0, w_conv, b_conv, wp, b_a, b_x, lam, k0s):
    m = z.shape[0]
    d_rnn = z.shape[1] // 2
    tm = _pick(m, (128, 64, 32, 16))
    rows = lambda c: pl.BlockSpec((tm, d_rnn), lambda i: (i, c))
    full = lambda shape: pl.BlockSpec(shape, lambda i: (0,) * len(shape))
    return pl.pallas_call(
        functools.partial(_lru_step_kernel, k0s=k0s),
        grid=(m // tm,),
        in_specs=[
            rows(0), rows(1), rows(0), rows(0), rows(0), rows(0),
            full((4, d_rnn)), full((1, d_rnn)), full(wp.shape),
            full((1, d_rnn)), full((1, d_rnn)), full((1, d_rnn)),
        ],
        out_specs=[rows(0), rows(0)],
        out_shape=[jax.ShapeDtypeStruct((m, d_rnn), BF16), jax.ShapeDtypeStruct((m, d_rnn), F32)],
        scratch_shapes=[pltpu.VMEM((tm, d_rnn), F32), pltpu.VMEM((tm, d_rnn), BF16)],
        compiler_params=_cparams(("arbitrary",)),
        name="lru_step",
    )(z, z, prev[0], prev[1], prev[2], h0, w_conv, b_conv.reshape(1, -1), wp,
      b_a.reshape(1, -1), b_x.reshape(1, -1), lam.reshape(1, -1))


def _ln_silu(v, g, b):
    mu = jnp.mean(v, axis=-1, keepdims=True)
    dv = v - mu
    var = jnp.mean(dv * dv, axis=-1, keepdims=True)
    return _silu(dv * lax.rsqrt(var + EPS) * g + b)


def _conf_seq_kernel(u_ref, w_ref, b_ref, g_ref, bl_ref, o_ref, xp_ref, v_ref, *, tt, width, pad):
    d = u_ref.shape[1]

    @pl.when(pl.program_id(1) == 0)
    def _():
        xp_ref[0:pad, :] = jnp.zeros((pad, d), F32)

    xp_ref[pad:, :] = u_ref[...]
    for c in range(d // LANES):
        c0, c1 = LANES * c, LANES * (c + 1)
        v_ref[:, c0:c1] = b_ref[:, c0:c1] + _causal_taps(xp_ref[:, c0:c1], w_ref, c0, c1, tt, pad - (width - 1))
    xp_ref[0:pad, :] = xp_ref[tt:tt + pad, :]
    o_ref[...] = _ln_silu(v_ref[...], g_ref[...], bl_ref[...]).astype(BF16)


def _conf_seq(u, batch, seq, w_dw, b_dw, g_ln, b_ln):
    d = u.shape[1]
    width = w_dw.shape[0]
    pad = -(-(width - 1) // SUBLANES) * SUBLANES
    tt = _pick(seq, (128, 64, 32))
    nt = seq // tt
    full = lambda shape: pl.BlockSpec(shape, lambda b, t: (0,) * len(shape))
    return pl.pallas_call(
        functools.partial(_conf_seq_kernel, tt=tt, width=width, pad=pad),
        grid=(batch, nt),
        in_specs=[
            pl.BlockSpec((tt, d), lambda b, t: (b * nt + t, 0)),
            full((width, d)), full((1, d)), full((1, d)), full((1, d)),
        ],
        out_specs=pl.BlockSpec((tt, d), lambda b, t: (b * nt + t, 0)),
        out_shape=jax.ShapeDtypeStruct((batch * seq, d), BF16),
        scratch_shapes=[pltpu.VMEM((tt + pad, d), F32), pltpu.VMEM((tt, d), F32)],
        compiler_params=_cparams(("arbitrary", "arbitrary")),
        name="conf_seq",
    )(u, w_dw, b_dw.reshape(1, d), g_ln.reshape(1, d), b_ln.reshape(1, d))


def _conf_step_kernel(u_ref, buf_ref, w_ref, b_ref, g_ref, bl_ref, o_ref, *, width):
    acc = b_ref[...] + w_ref[width - 1:width, :] * u_ref[...]
    for k in range(width - 1):
        acc = acc + w_ref[k:k + 1, :] * buf_ref[k]
    o_ref[...] = _ln_silu(acc, g_ref[...], bl_ref[...]).astype(BF16)


def _conf_step(u, buf_t, w_dw, b_dw, g_ln, b_ln):
    m, d = u.shape
    width = w_dw.shape[0]
    tm = _pick(m, (32, 16, 8))
    full = lambda shape: pl.BlockSpec(shape, lambda i: (0,) * len(shape))
    return pl.pallas_call(
        functools.partial(_conf_step_kernel, width=width),
        grid=(m // tm,),
        in_specs=[
            pl.BlockSpec((tm, d), lambda i: (i, 0)),
            pl.BlockSpec((width - 1, tm, d), lambda i: (0, i, 0)),
            full((width, d)), full((1, d)), full((1, d)), full((1, d)),
        ],
        out_specs=pl.BlockSpec((tm, d), lambda i: (i, 0)),
        out_shape=jax.ShapeDtypeStruct((m, d), BF16),
        compiler_params=_cparams(("arbitrary",)),
        name="conf_step",
    )(u, buf_t, w_dw, b_dw.reshape(1, d), g_ln.reshape(1, d), b_ln.reshape(1, d))


def _rows_dense(rows, tn):
    sub = lax.broadcasted_iota(I32, (SUBLANES, tn), 0)
    out = jnp.zeros((SUBLANES, tn), rows[0].dtype)
    for k, r in enumerate(rows):
        out = jnp.where(sub == k, r, out)
    return out


def _expert_iota(rows, tn, first):
    return lax.broadcasted_iota(I32, (rows, tn), 0).astype(F32) + float(first)


def _router_kernel(x_ref, wrt_ref, br_ref, idx_ref, wts_ref, rank_ref, cnt_ref, carry_ref, *, tn, n_exp):
    @pl.when(pl.program_id(0) == 0)
    def _():
        carry_ref[...] = jnp.zeros_like(carry_ref)

    x = _unpack_bf16_pairs(x_ref[...])
    logits = lax.dot_general(wrt_ref[...], x, (((1,), (1,)), ((), ())), preferred_element_type=F32)
    probs = _sigmoid(logits)
    choice = probs + br_ref[...]
    row = _expert_iota(n_exp, tn, 0)
    neg = jnp.float32(-jnp.inf)
    big = jnp.float32(n_exp)
    gsz = n_exp // N_GROUPS

    score = []
    for g in range(N_GROUPS):
        v = choice[g * gsz:(g + 1) * gsz, :]
        rg = _expert_iota(gsz, tn, g * gsz)
        m1 = jnp.max(v, axis=0, keepdims=True)
        i1 = jnp.min(jnp.where(v == m1, rg, big), axis=0, keepdims=True)
        m2 = jnp.max(jnp.where(rg == i1, neg, v), axis=0, keepdims=True)
        score.append(m1 + m2)

    masked = []
    for g in range(N_GROUPS):
        beaten = jnp.zeros((1, tn), I32)
        for o in range(N_GROUPS):
            if o == g:
                continue
            wins = (score[o] >= score[g]) if o < g else (score[o] > score[g])
            beaten = beaten + wins.astype(I32)
        masked.append(jnp.where(beaten < TOPK_GROUPS, choice[g * gsz:(g + 1) * gsz, :], neg))
    masked = jnp.concatenate(masked, axis=0)

    idx, pk = [], []
    onehot = jnp.zeros((n_exp, tn), F32)
    for _ in range(TOP_K):
        m = jnp.max(masked, axis=0, keepdims=True)
        ik = jnp.min(jnp.where(masked == m, row, big), axis=0, keepdims=True)
        hit = row == ik
        pk.append(jnp.sum(jnp.where(hit, probs, 0.0), axis=0, keepdims=True))
        idx.append(ik)
        onehot = jnp.where(hit, 1.0, onehot)
        masked = jnp.where(hit, neg, masked)

    total = pk[0]
    for p in pk[1:]:
        total = total + p
    wts = [p / total * ROUTED_SCALE for p in pk]

    r_i = lax.broadcasted_iota(I32, (tn, tn), 0)
    c_i = lax.broadcasted_iota(I32, (tn, tn), 1)
    tri = jnp.where(r_i < c_i, 1.0, 0.0).astype(BF16)
    before = _dot(onehot.astype(BF16), tri) + carry_ref[...]
    rank = [jnp.sum(jnp.where(row == ik, before, 0.0), axis=0, keepdims=True).astype(I32) for ik in idx]
    carry_ref[...] = carry_ref[...] + jnp.sum(onehot, axis=1, keepdims=True)

    idx_ref[...] = _rows_dense(idx, tn).astype(I32)
    wts_ref[...] = _rows_dense(wts, tn)
    rank_ref[...] = _rows_dense(rank, tn)
    cnt_ref[...] = carry_ref[...]


def _token_tile(n):
    return _pick(n, (640, 512, 384, 256, 128))


def _router(h2p, w_router_t, b_router):
    n, dh = h2p.shape
    n_exp = w_router_t.shape[0]
    tn = _token_tile(n)
    row8 = pl.BlockSpec((TOP_K, tn), lambda i: (0, i))
    return pl.pallas_call(
        functools.partial(_router_kernel, tn=tn, n_exp=n_exp),
        grid=(n // tn,),
        in_specs=[
            pl.BlockSpec((tn, dh), lambda i: (i, 0)),
            pl.BlockSpec((n_exp, 2 * dh), lambda i: (0, 0)),
            pl.BlockSpec((n_exp, 1), lambda i: (0, 0)),
        ],
        out_specs=[row8, row8, row8, pl.BlockSpec((n_exp, 1), lambda i: (0, 0))],
        out_shape=[
            jax.ShapeDtypeStruct((TOP_K, n), I32),
            jax.ShapeDtypeStruct((TOP_K, n), F32),
            jax.ShapeDtypeStruct((TOP_K, n), I32),
            jax.ShapeDtypeStruct((n_exp, 1), F32),
        ],
        scratch_shapes=[pltpu.VMEM((n_exp, 1), F32)],
        compiler_params=_cparams(("arbitrary",)),
        name="moe_router",
    )(h2p, w_router_t, b_router.reshape(n_exp, 1))


def _dest_kernel(idx_ref, rank_ref, ps_ref, o_ref, *, tn, n_exp):
    row = lax.broadcasted_iota(I32, (n_exp, tn), 0)
    idx = idx_ref[...]
    start = []
    for k in range(TOP_K):
        hit = row == idx[k:k + 1, :]
        start.append(jnp.sum(jnp.where(hit, ps_ref[...], 0.0), axis=0, keepdims=True).astype(I32))
    o_ref[...] = _rows_dense(start, tn) + rank_ref[...]


def _dest_rows(idx, rank, pstarts):
    n = idx.shape[1]
    n_exp = pstarts.shape[0]
    tn = _token_tile(n)
    row8 = pl.BlockSpec((TOP_K, tn), lambda i: (0, i))
    return pl.pallas_call(
        functools.partial(_dest_kernel, tn=tn, n_exp=n_exp),
        grid=(n // tn,),
        in_specs=[row8, row8, pl.BlockSpec((n_exp, 1), lambda i: (0, 0))],
        out_specs=row8,
        out_shape=jax.ShapeDtypeStruct((TOP_K, n), I32),
        compiler_params=_cparams(("arbitrary",)),
        name="moe_dest",
    )(idx, rank, pstarts)


def _dispatch_kernel(dest_ref, x_ref, wsg_ref, wsu_ref, wsd_ref, xs_hbm, sh_ref, wgb_ref, wub_ref, wdb_ref,
                     sem, *, tm):
    @pl.when(pl.program_id(0) == 0)
    def _():
        wgb_ref[...] = wsg_ref[...].astype(BF16)
        wub_ref[...] = wsu_ref[...].astype(BF16)
        wdb_ref[...] = wsd_ref[...].astype(BF16)

    def scatter_rows(group):
        for t in range(group * tm // DISPATCH_GROUPS, (group + 1) * tm // DISPATCH_GROUPS):
            for k in range(TOP_K):
                d = dest_ref[0, t * TOP_K + k]
                pltpu.make_async_copy(x_ref.at[pl.ds(t, 1)], xs_hbm.at[pl.ds(d, 1)], sem).start(priority=k % 2)

    x = _unpack_bf16_pairs(x_ref[...])
    de, d = wdb_ref.shape
    group = 0
    halves = []
    for w_ref in (wgb_ref, wub_ref):
        parts = []
        for c in range(2):
            scatter_rows(group)
            group += 1
            parts.append(_dot(x, w_ref[:, c * de // 2:(c + 1) * de // 2]))
        halves.append(jnp.concatenate(parts, axis=1))
    act = (_silu(halves[0]) * halves[1]).astype(BF16)
    for c in range(4):
        scatter_rows(group)
        group += 1
        sh_ref[:, c * d // 4:(c + 1) * d // 4] = _dot(act, wdb_ref[:, c * d // 4:(c + 1) * d // 4])
    assert group == DISPATCH_GROUPS

    for k in range(TOP_K):
        pltpu.make_async_copy(x_ref, xs_hbm.at[pl.ds(0, tm)], sem).wait()


def _dispatch_shared(h2p, dest, w_sg, w_su, w_sd, layer, n_rows):
    n, dh = h2p.shape
    d = 2 * dh
    de = w_sg.shape[2]
    tm = _pick(n, (320, 256, 176, 128, 64, 48, 32, 16))
    nt = n // tm
    xs, shared = pl.pallas_call(
        functools.partial(_dispatch_kernel, tm=tm),
        grid=(nt,),
        in_specs=[
            pl.BlockSpec((None, 1, tm * TOP_K), lambda i: (i, 0, 0), memory_space=pltpu.SMEM),
            pl.BlockSpec((tm, dh), lambda i: (i, 0)),
            pl.BlockSpec((None, d, de), lambda i: (layer, 0, 0)),
            pl.BlockSpec((None, d, de), lambda i: (layer, 0, 0)),
            pl.BlockSpec((None, de, d), lambda i: (layer, 0, 0)),
        ],
        out_specs=[
            pl.BlockSpec(memory_space=pl.ANY),
            pl.BlockSpec((tm, d), lambda i: (i, 0)),
        ],
        out_shape=[
            jax.ShapeDtypeStruct((n_rows, dh), U32),
            jax.ShapeDtypeStruct((n, d), F32),
        ],
        scratch_shapes=[
            pltpu.VMEM((d, de), BF16), pltpu.VMEM((d, de), BF16), pltpu.VMEM((de, d), BF16),
            pltpu.SemaphoreType.DMA(()),
        ],
        compiler_params=_cparams(("arbitrary",)),
        name="moe_dispatch",
    )(dest.reshape(nt, 1, tm * TOP_K), h2p, w_sg, w_su, w_sd)
    return xs, shared


def _expert_kernel(be_ref, bv_ref, na_ref, sz_ref, xs_ref, wg_hbm, wu_hbm, wd_hbm, y_ref,
                   wg_buf, wu_buf, wd_buf, wgb_ref, wub_ref, wdb_ref, sem, slot_ref, *, layer, n_exp):
    i = pl.program_id(0)
    e = be_ref[i]

    def weight_copies(expert, slot):
        copies = []
        for w, (hbm, buf) in enumerate(((wg_hbm, wg_buf), (wu_hbm, wu_buf), (wd_hbm, wd_buf))):
            rows = buf.shape[1] // WEIGHT_DMA_CHUNKS
            for c in range(WEIGHT_DMA_CHUNKS):
                part = pl.ds(c * rows, rows)
                copies.append(pltpu.make_async_copy(hbm.at[layer, expert, part], buf.at[slot, part], sem.at[slot, w]))
        return copies

    def start_weights(expert, slot):
        copies = weight_copies(expert, slot)
        for n, c in enumerate(copies):
            c.start(priority=0 if n < len(copies) // 3 else 1)

    @pl.when(i == 0)
    def _():
        slot_ref[0] = 0
        start_weights(e, 0)

    is_first = (i == 0) | (e != be_ref[jnp.maximum(i - 1, 0)])

    @pl.when(is_first & (i < na_ref[0]))
    def _():
        slot = slot_ref[0]
        nxt = lax.while_loop(lambda n: (n < n_exp) & (sz_ref[jnp.minimum(n, n_exp - 1)] == 0),
                             lambda n: n + 1, e + 1)

        @pl.when(nxt < n_exp)
        def _():
            start_weights(nxt, 1 - slot)

        for c in weight_copies(e, slot):
            c.wait()
        wgb_ref[...] = wg_buf[slot].astype(BF16)
        wub_ref[...] = wu_buf[slot].astype(BF16)
        wdb_ref[...] = wd_buf[slot].astype(BF16)
        slot_ref[0] = 1 - slot

    @pl.when(i < na_ref[0])
    def _():
        x = _unpack_bf16_pairs(xs_ref[...])
        row = lax.broadcasted_iota(I32, x.shape, 0)
        x = jnp.where(row < bv_ref[i], x, jnp.zeros_like(x))
        act = _silu(_dot(x, wgb_ref[...])) * _dot(x, wub_ref[...])
        y_ref[...] = _dot(act.astype(BF16), wdb_ref[...])


def _experts(xs, blk_exp, blk_valid, n_active, sizes, w_eg, w_eu, w_ed, layer):
    n_rows, dh = xs.shape
    d = 2 * dh
    n_exp, _, de = w_eg.shape[1:]
    nb = n_rows // ROW_BLOCK
    blk = lambda i, be, bv, na, sz: (jnp.minimum(i, na[0] - 1), 0)
    grid_spec = pltpu.PrefetchScalarGridSpec(
        num_scalar_prefetch=4,
        grid=(nb,),
        in_specs=[
            pl.BlockSpec((ROW_BLOCK, dh), blk),
            pl.BlockSpec(memory_space=pl.ANY),
            pl.BlockSpec(memory_space=pl.ANY),
            pl.BlockSpec(memory_space=pl.ANY),
        ],
        out_specs=pl.BlockSpec((ROW_BLOCK, d), blk),
        scratch_shapes=[
            pltpu.VMEM((2, d, de), F32), pltpu.VMEM((2, d, de), F32), pltpu.VMEM((2, de, d), F32),
            pltpu.VMEM((d, de), BF16), pltpu.VMEM((d, de), BF16), pltpu.VMEM((de, d), BF16),
            pltpu.SemaphoreType.DMA((2, 3)),
            pltpu.SMEM((1,), I32),
        ],
    )
    return pl.pallas_call(
        functools.partial(_expert_kernel, layer=layer, n_exp=n_exp),
        grid_spec=grid_spec,
        out_shape=jax.ShapeDtypeStruct((n_rows, d), F32),
        compiler_params=_cparams(("arbitrary",)),
        name="moe_experts",
    )(blk_exp, blk_valid, n_active, sizes, xs, w_eg, w_eu, w_ed)


COMBINE_ROWS = 16
WEIGHT_DMA_CHUNKS = 1
DISPATCH_GROUPS = 8


def _combine_kernel(dest_ref, dnext_ref, y_hbm, wts_ref, sh_ref, x_ref, gate_ref, *rest, tm, post):
    if post == "mix":
        g_ref, shift_ref, scale_ref, o_ref, h_ref, gbuf, sem = rest
    elif post == "final":
        g_ref, h_ref, gbuf, sem = rest
    else:
        o_ref, gbuf, sem = rest
    i = pl.program_id(0)
    slot = lax.rem(i, 2)

    def start_row(dref, s, t):
        for k in range(TOP_K):
            pltpu.make_async_copy(y_hbm.at[pl.ds(dref[0, t * TOP_K + k], 1)], gbuf.at[s, k, pl.ds(t, 1)],
                                  sem.at[s]).start(priority=k % 2)

    def wait_slot(s):
        for k in range(TOP_K):
            pltpu.make_async_copy(y_hbm.at[pl.ds(0, tm)], gbuf.at[s, k], sem.at[s]).wait()

    @pl.when(i == 0)
    def _():
        def body(t, carry):
            start_row(dest_ref, 0, t)
            return carry

        lax.fori_loop(0, tm, body, 0)

    wait_slot(slot)

    def rows_of(ref, rows):
        return ref[...] if ref.shape[0] == 1 else ref[rows, :]

    for q in range(tm // COMBINE_ROWS):
        for t in range(q * COMBINE_ROWS, (q + 1) * COMBINE_ROWS):
            start_row(dnext_ref, 1 - slot, t)
        rows = slice(q * COMBINE_ROWS, (q + 1) * COMBINE_ROWS)
        w = wts_ref[rows, :]
        f = sh_ref[rows, :]
        for k in range(TOP_K):
            f = f + w[:, k:k + 1] * gbuf[slot, k, rows, :]
        xn = x_ref[rows, :] + rows_of(gate_ref, rows) * f
        if post != "final":
            o_ref[rows, :] = xn
        if post is not None:
            y = xn * lax.rsqrt(jnp.mean(xn * xn, axis=-1, keepdims=True) + EPS) * g_ref[...]
            if post == "mix":
                y = y * (1.0 + rows_of(scale_ref, rows)) + rows_of(shift_ref, rows)
            h_ref[rows, :] = y.astype(h_ref.dtype)

    @pl.when(i == pl.num_programs(0) - 1)
    def _():
        wait_slot(1 - slot)


def _combine(y_rows, dest, wts, shared, x, gate, seq, row_off, post=None, g=None, shift=None, scale=None):
    m, d = x.shape
    tm = _pick(m if seq == 1 else seq, (128, 64, 32, 16))
    off = row_off // tm
    nt = m // tm
    dest3 = dest[row_off:row_off + m].reshape(nt, 1, tm * TOP_K)
    smem = lambda nxt: pl.BlockSpec((None, 1, tm * TOP_K), lambda i: (jnp.minimum(i + nxt, nt - 1), 0, 0),
                                    memory_space=pltpu.SMEM)
    tile = pl.BlockSpec((tm, d), lambda i: (i, 0))
    in_specs = [
        smem(0), smem(1),
        pl.BlockSpec(memory_space=pl.ANY),
        pl.BlockSpec((tm, TOP_K), lambda i: (i + off, 0)),
        pl.BlockSpec((tm, d), lambda i: (i + off, 0)),
        tile,
        _mod_spec(seq, tm, d, False),
    ]
    args = [dest3, dest3, y_rows, wts, shared, x, gate]
    x_out = jax.ShapeDtypeStruct((m, d), F32)
    if post == "mix":
        in_specs += [pl.BlockSpec((1, d), lambda i: (0, 0)), _mod_spec(seq, tm, d, False), _mod_spec(seq, tm, d, False)]
        args += [g.reshape(1, d), shift, scale]
        out_specs, out_shape = [tile, tile], [x_out, jax.ShapeDtypeStruct((m, d), BF16)]
    elif post == "final":
        in_specs += [pl.BlockSpec((1, d), lambda i: (0, 0))]
        args += [g.reshape(1, d)]
        out_specs, out_shape = tile, x_out
    else:
        out_specs, out_shape = tile, x_out
    return pl.pallas_call(
        functools.partial(_combine_kernel, tm=tm, post=post),
        grid=(nt,),
        in_specs=in_specs,
        out_specs=out_specs,
        out_shape=out_shape,
        scratch_shapes=[pltpu.VMEM((2, TOP_K, tm, d), F32), pltpu.SemaphoreType.DMA((2,))],
        compiler_params=_cparams(("arbitrary",)),
        name="moe_combine",
    )(*args)


def _moe(h2p, w_router_b, b_router, w_eg, w_eu, w_ed, w_sg, w_su, w_sd, layer):
    n = h2p.shape[0]
    n_exp = w_router_b.shape[0]
    idx, wts, rank, counts = _router(h2p, w_router_b, b_router)

    sizes = counts[:, 0].astype(I32)
    padded = (sizes + ROW_BLOCK - 1) // ROW_BLOCK * ROW_BLOCK
    pends = jnp.cumsum(padded)
    pstarts = pends - padded
    n_blocks = -(-(n * TOP_K) // ROW_BLOCK) + n_exp
    n_active = (pends[-1] // ROW_BLOCK).astype(I32)
    experts = jnp.arange(n_exp, dtype=I32)
    last_exp = jnp.max(jnp.where(sizes > 0, experts, 0))
    blk_start = jnp.arange(n_blocks, dtype=I32) * ROW_BLOCK
    blk_exp = jnp.minimum(jnp.sum(pends[None, :] <= blk_start[:, None], axis=1).astype(I32), last_exp)
    seg_end = jnp.sum(jnp.where(blk_exp[:, None] == experts[None, :], (pstarts + sizes)[None, :], 0), axis=1)
    blk_valid = jnp.clip(seg_end - blk_start, 0, ROW_BLOCK).astype(I32)
    dest = _dest_rows(idx, rank, pstarts.astype(F32).reshape(n_exp, 1)).T

    xs, shared = _dispatch_shared(h2p, dest, w_sg, w_su, w_sd, layer, n_blocks * ROW_BLOCK)
    y_rows = _experts(xs, blk_exp, blk_valid, n_active.reshape(1), sizes, w_eg, w_eu, w_ed, layer)
    return y_rows, dest, wts.T, shared


def _last_rows(x, batch, seq, n, col0=0):
    return jnp.stack([x[(b + 1) * seq - n:(b + 1) * seq, col0:] for b in range(batch)])


def _group_mods(mod_l, b0, batch, seq, d):
    m = mod_l[b0:b0 + batch].reshape(batch, N_MOD, d)
    if seq > 1:
        return [m[:, k, :].reshape(batch, 1, d) for k in range(N_MOD)]
    return [m[:, k, :].reshape(1, batch, d) for k in range(N_MOD)]


def kernel(x_prompt, x_sample, state_lru_h, state_lru_conv, state_conv, c_prompt, c_sample, w_ada, b_ada, g_mix, g_ffn, w_lru_in, w_lru_conv, b_lru_conv, w_lru_a, b_lru_a, w_lru_x, b_lru_x, lru_lambda, w_lru_out, w_pw1, b_pw1, w_dw, b_dw, g_cln, b_cln, w_pw2, b_pw2, w_router, b_router, w_exp_gate, w_exp_up, w_exp_down, w_sh_gate, w_sh_up, w_sh_down, g_final):
    bp, tp, d = x_prompt.shape
    bs, ts, _ = x_sample.shape
    assert ts == 1
    depth = w_ada.shape[0]
    d_rnn = w_lru_in.shape[2] // 2
    np_, ns = bp * tp, bs * ts

    c_all = jnp.concatenate([c_prompt, c_sample], axis=0)
    pad_rows = -c_all.shape[0] % SUBLANES
    c_all = jnp.pad(c_all, ((0, pad_rows), (0, 0)))
    mod = _ada_mod(c_all, w_ada, b_ada)

    xp = x_prompt.reshape(np_, d)
    xs = x_sample.reshape(ns, d)
    k0s = _gate_windows(d_rnn, d_rnn // N_LRU_BLOCKS)
    zero_b = jnp.zeros((d,), F32)

    hs_p, hs_s, lcs_p, lcs_s, cbs_p, cbs_s = [], [], [], [], [], []
    all_mods = [(_group_mods(mod[l], 0, bp, tp, d), _group_mods(mod[l], bp, bs, ts, d)) for l in range(depth)]
    hp = _norm_mod(xp, g_mix[0], all_mods[0][0][0], all_mods[0][0][1], tp, False)
    hsm = _norm_mod(xs, g_mix[0], all_mods[0][1][0], all_mods[0][1][1], ts, False)
    for l in range(depth):
        mods_p, mods_s = all_mods[l]
        j = l // 2
        if l % 2 == 0:
            wp = _pack_gate_weights(w_lru_a[j], w_lru_x[j])
            lru_args = (w_lru_conv[j], b_lru_conv[j], wp, b_lru_a[j], b_lru_x[j], lru_lambda[j], k0s)
            zp = _mm_plain(hp, w_lru_in, j, (768, 896, 512, 384, 256, 128))
            zs = _mm_plain(hsm, w_lru_in, j, (768, 896, 512, 384, 256, 128))
            ygp, hlp = _lru_seq(zp, bp, tp, *lru_args)
            buf = state_lru_conv[j]
            ygs, hls = _lru_step(zs, (buf[:, 0], buf[:, 1], buf[:, 2]), state_lru_h[j], *lru_args)
            hs_p.append(hlp)
            hs_s.append(hls)
            lcs_p.append(_last_rows(zp, bp, tp, 3, d_rnn))
            lcs_s.append(jnp.concatenate([buf[:, 1:], zs[:, None, d_rnn:]], axis=1))
            xp = _mm_res(ygp, w_lru_out, zero_b, xp, mods_p[2], j, tp)
            xs = _mm_res(ygs, w_lru_out, zero_b, xs, mods_s[2], j, ts)
        else:
            up = _mm_glu(hp, w_pw1, b_pw1, j)
            us = _mm_glu(hsm, w_pw1, b_pw1, j)
            conf_args = (w_dw[j], b_dw[j], g_cln[j], b_cln[j])
            vp = _conf_seq(up, bp, tp, *conf_args)
            buf = state_conv[j]
            vs = _conf_step(us, jnp.transpose(buf, (1, 0, 2)), *conf_args)
            width = w_dw.shape[1]
            cbs_p.append(_last_rows(up, bp, tp, width - 1))
            cbs_s.append(jnp.concatenate([buf[:, 1:], us[:, None, :]], axis=1))
            xp = _mm_res(vp, w_pw2, b_pw2[j], xp, mods_p[2], j, tp)
            xs = _mm_res(vs, w_pw2, b_pw2[j], xs, mods_s[2], j, ts)

        h2p = jnp.concatenate([
            _norm_mod(xp, g_ffn[l], mods_p[3], mods_p[4], tp, True),
            _norm_mod(xs, g_ffn[l], mods_s[3], mods_s[4], ts, True),
        ], axis=0)
        y_rows, dest, wts, shared = _moe(h2p, w_router[l].T.astype(BF16), b_router[l], w_exp_gate, w_exp_up,
                                         w_exp_down, w_sh_gate, w_sh_up, w_sh_down, l)
        moe_p = (y_rows, dest, wts, shared, xp, mods_p[5], tp, 0)
        moe_s = (y_rows, dest, wts, shared, xs, mods_s[5], ts, np_)
        if l + 1 < depth:
            nxt_p, nxt_s = all_mods[l + 1]
            xp, hp = _combine(*moe_p, post="mix", g=g_mix[l + 1], shift=nxt_p[0], scale=nxt_p[1])
            xs, hsm = _combine(*moe_s, post="mix", g=g_mix[l + 1], shift=nxt_s[0], scale=nxt_s[1])
        else:
            y_prompt = _combine(*moe_p, post="final", g=g_final).reshape(bp, tp, d)
            y_sample = _combine(*moe_s, post="final", g=g_final).reshape(bs, ts, d)

    return (y_prompt, y_sample, jnp.stack(hs_p), jnp.stack(lcs_p), jnp.stack(cbs_p),
            jnp.stack(hs_s), jnp.stack(lcs_s), jnp.stack(cbs_s))
```
